```python
import functools
import jax, jax.numpy as jnp
from jax import lax
import numpy as np

D_MODEL = 2048
BATCH = 4
SEQ = 2048
DEPTH = 4
DEC_BATCH = 8
DEC_SEQ = 1
PAST_LEN = 16384
PAGE_SIZE = 128

HEAD_DIM = 128
WIDTH_A = D_MODEL // 2
N_HEADS_A = WIDTH_A // HEAD_DIM
DILATIONS = ((128, 1), (512, 4), (2048, 16))
MAX_WINDOW = 2048
N_HEADS_B = 8
DV_B = (D_MODEL // 2) // N_HEADS_B
DK_B = DV_B // 2
WIDTH_BQK = N_HEADS_B * DK_B
WIDTH_BV = N_HEADS_B * DV_B
MIX_WIDTH = WIDTH_A + WIDTH_BV
IN_WIDTH = 3 * WIDTH_A + 2 * WIDTH_BQK + 2 * WIDTH_BV
FFN_DIM = 5632
PLE_DIM = 256
RET_CHUNK = 128
LN_EPS = 1e-5
GN_EPS = 1e-5
NEG_INF = -1e30
DEEPNORM_ALPHA = (2 * DEPTH) ** 0.25
DEEPNORM_BETA = (8 * DEPTH) ** -0.25

kernel_name = "hybrid_dilated_attn_retention_decoder_step"


def alibi_slopes():
    h = jnp.arange(1, N_HEADS_A + 1, dtype=jnp.float32)
    return jnp.exp2(-8.0 * h / N_HEADS_A)


def retention_log_gamma():
    h = jnp.arange(N_HEADS_B, dtype=jnp.float32)
    return jnp.log(1.0 - jnp.exp2(-5.0 - h))


def layer_norm(x, g, b):
    xf = x.astype(jnp.float32)
    mu = xf.mean(-1, keepdims=True)
    var = jnp.mean(jnp.square(xf - mu), -1, keepdims=True)
    return ((xf - mu) * lax.rsqrt(var + LN_EPS) * g + b).astype(x.dtype)


def swiglu(x, w1, w3, w2):
    return (jax.nn.silu(x @ w1) * (x @ w3)) @ w2


def split_projection(h, w_in):
    B, T, _ = h.shape
    z = h @ w_in
    bounds = np.cumsum([WIDTH_A, WIDTH_A, WIDTH_A, WIDTH_BQK, WIDTH_BQK, WIDTH_BV]).tolist()
    qa, ka, va, qb, kb, vb, gb = jnp.split(z, bounds, axis=-1)
    heads = lambda a, n: a.reshape(B, T, n, -1)
    return (heads(qa, N_HEADS_A), heads(ka, N_HEADS_A), heads(va, N_HEADS_A),
            heads(qb, N_HEADS_B), heads(kb, N_HEADS_B) * (DK_B ** -0.5), heads(vb, N_HEADS_B), gb)


def strided_band_attention(q, k, v, window, dil):
    B, T, H, Dh = q.shape
    steps = window // dil
    L = T // dil
    C = steps
    nb = -(-L // C)
    Lp = nb * C

    def to_blocks(a):
        a = a.reshape(B, L, dil, H, Dh).transpose(0, 2, 1, 3, 4)
        a = jnp.pad(a, ((0, 0), (0, 0), (0, Lp - L), (0, 0), (0, 0)))
        return a.reshape(B, dil, nb, C, H, Dh)

    def with_prev(a):
        prev = jnp.pad(a[:, :, :-1], ((0, 0), (0, 0), (1, 0), (0, 0), (0, 0), (0, 0)))
        return jnp.concatenate([prev, a], axis=3)

    qs = to_blocks(q)
    kb = with_prev(to_blocks(k))
    vb = with_prev(to_blocks(v))
    s = jnp.einsum('brnqhd,brnkhd->brnhqk', qs, kb,
                   preferred_element_type=jnp.float32) * (HEAD_DIM ** -0.5)
    qi = jnp.arange(C)[:, None]
    ki = jnp.arange(2 * C)[None, :]
    dist = qi + C - ki
    blk = jnp.arange(nb)[:, None, None]
    valid = (dist >= 0) & (dist <= steps) & (blk * C + ki - C >= 0)
    bias = -alibi_slopes()[:, None, None] * (dist * dil).astype(jnp.float32)
    s = jnp.where(valid[:, None], s + bias, NEG_INF)
    lse = jax.nn.logsumexp(s, axis=-1)
    p = jnp.exp(s - lse[..., None])
    o = jnp.einsum('brnhqk,brnkhd->brnqhd', p.astype(v.dtype), vb)
    o = o.reshape(B, dil, Lp, H, Dh)[:, :, :L].transpose(0, 2, 1, 3, 4).reshape(B, T, H, Dh)
    lse = lse.transpose(0, 1, 2, 4, 3).reshape(B, dil, Lp, H)[:, :, :L]
    lse = lse.transpose(0, 2, 1, 3).reshape(B, T, H)
    return o, lse


def combine_dilations(outs, lses, dtype):
    w = jax.nn.softmax(jnp.stack(lses, 0), axis=0)
    o = jnp.einsum('gbth,gbthd->bthd', w, jnp.stack(outs, 0).astype(jnp.float32))
    return o.astype(dtype)


def dilated_attention_prompt(q, k, v):
    outs, lses = [], []
    for window, dil in DILATIONS:
        o, lse = strided_band_attention(q, k, v, window, dil)
        outs.append(o)
        lses.append(lse)
    return combine_dilations(outs, lses, q.dtype)


def dilated_attention_decode(q, k_all, v_all, n_past):
    S = q.shape[1]
    outs, lses = [], []
    for window, dil in DILATIONS:
        steps = window // dil
        j = jnp.arange(steps + 1)
        idx = n_past + jnp.arange(S)[:, None] - j[None, :] * dil
        valid = idx >= 0
        idx = jnp.maximum(idx, 0)
        kg = k_all[:, idx]
        vg = v_all[:, idx]
        s = jnp.einsum('bshd,bsjhd->bhsj', q, kg,
                       preferred_element_type=jnp.float32) * (HEAD_DIM ** -0.5)
        s = s - alibi_slopes()[:, None, None] * (j * dil).astype(jnp.float32)[None, None, :]
        s = jnp.where(valid[None, None], s, NEG_INF)
        lse = jax.nn.logsumexp(s, axis=-1)
        p = jnp.exp(s - lse[..., None])
        outs.append(jnp.einsum('bhsj,bsjhd->bshd', p.astype(vg.dtype), vg))
        lses.append(lse.transpose(0, 2, 1))
    return combine_dilations(outs, lses, q.dtype)


def retention(q, k, v, s0):
    B, T, H, _ = q.shape
    dv = v.shape[-1]
    C = RET_CHUNK if T % RET_CHUNK == 0 else T
    n = T // C
    log_g = retention_log_gamma()

    def chunks(a):
        return a.astype(jnp.float32).reshape(B, n, C, H, a.shape[-1]).transpose(1, 0, 2, 3, 4)

    pos = jnp.arange(C, dtype=jnp.float32)
    rel = pos[:, None] - pos[None, :]
    decay_mask = jnp.where(rel >= 0, jnp.exp(jnp.maximum(rel, 0.0)[None] * log_g[:, None, None]), 0.0)
    q_decay = jnp.exp((pos[:, None] + 1.0) * log_g[None])
    k_decay = jnp.exp((C - 1.0 - pos[:, None]) * log_g[None])
    chunk_decay = jnp.exp(C * log_g)

    def step(state, inp):
        qc, kc, vc = inp
        inner = jnp.einsum('bihd,bjhd->bhij', qc, kc) * decay_mask
        o = (jnp.einsum('bhij,bjhe->bihe', inner, vc)
             + jnp.einsum('bihd,bhde->bihe', qc, state) * q_decay[None, :, :, None])
        state = (state * chunk_decay[None, :, None, None]
                 + jnp.einsum('bjhd,bjhe->bhde', kc * k_decay[None, :, :, None], vc))
        return state, o

    s_final, o = lax.scan(step, s0.astype(jnp.float32), (chunks(q), chunks(k), chunks(v)))
    return o.transpose(1, 0, 2, 3, 4).reshape(B, T, H, dv), s_final


def retention_readout(o, gb, gn_g, gn_b):
    B, T, H, dv = o.shape
    mu = o.mean(-1, keepdims=True)
    var = jnp.mean(jnp.square(o - mu), -1, keepdims=True)
    on = ((o - mu) * lax.rsqrt(var + GN_EPS)).reshape(B, T, H * dv)
    return ((on * gn_g + gn_b) * jax.nn.silu(gb.astype(jnp.float32))).astype(gb.dtype)


def merge_heads(oa, ob, w_out):
    B, T = oa.shape[:2]
    return jnp.concatenate([oa.reshape(B, T, WIDTH_A), ob], axis=-1) @ w_out


def mix_prompt(h, w_in, w_out, gn_g, gn_b):
    B, T, _ = h.shape
    qa, ka, va, qb, kb, vb, gb = split_projection(h, w_in)
    oa = dilated_attention_prompt(qa, ka, va)
    s0 = jnp.zeros((B, N_HEADS_B, DK_B, DV_B), jnp.float32)
    ob, s_final = retention(qb, kb, vb, s0)
    out = merge_heads(oa, retention_readout(ob, gb, gn_g, gn_b), w_out)
    keep = min(MAX_WINDOW, T)
    return out, (ka[:, T - keep:], va[:, T - keep:], s_final.astype(h.dtype))


def mix_sample(h, cache_k, cache_v, state, w_in, w_out, gn_g, gn_b):
    qa, ka, va, qb, kb, vb, gb = split_projection(h, w_in)
    n_past = cache_k.shape[1]
    k_all = jnp.concatenate([cache_k.astype(ka.dtype), ka], axis=1)
    v_all = jnp.concatenate([cache_v.astype(va.dtype), va], axis=1)
    oa = dilated_attention_decode(qa, k_all, v_all, n_past)
    ob, s_final = retention(qb, kb, vb, state)
    out = merge_heads(oa, retention_readout(ob, gb, gn_g, gn_b), w_out)
    return out, (ka, va, s_final.astype(h.dtype))


def decoder_layer(x, p, mix_fn, f1_w1, f1_w3, f1_w2, f2_w1, f2_w3, f2_w2, ln_g, ln_b, w_ple, w_gate):
    x = layer_norm(DEEPNORM_ALPHA * x + 0.5 * swiglu(x, f1_w1, f1_w3, f1_w2), ln_g[0], ln_b[0])
    m, new_state = mix_fn(x)
    x = layer_norm(DEEPNORM_ALPHA * x + m, ln_g[1], ln_b[1])
    x = layer_norm(DEEPNORM_ALPHA * x + 0.5 * swiglu(x, f2_w1, f2_w3, f2_w2), ln_g[2], ln_b[2])
    ple = (p @ w_ple) * jax.nn.sigmoid(x @ w_gate)
    x = layer_norm(DEEPNORM_ALPHA * x + ple, ln_g[3], ln_b[3])
    return x, new_state


def setup_inputs(seed: int = 0) -> dict:
    key = jax.random.key(seed)
    ks = jax.random.split(key, 24)
    f32 = jnp.float32
    nrm = lambda k, shape, s: jax.random.normal(k, shape, f32) * s
    win_buf = min(MAX_WINDOW, PAST_LEN)
    col_scale = np.ones((IN_WIDTH,), np.float32)
    col_scale[2 * WIDTH_A:3 * WIDTH_A] = DEEPNORM_BETA
    vb0 = 3 * WIDTH_A + 2 * WIDTH_BQK
    col_scale[vb0:vb0 + WIDTH_BV] = DEEPNORM_BETA
    return {
        "x_prompt": nrm(ks[0], (BATCH, SEQ, D_MODEL), 1.0),
        "x_sample": nrm(ks[1], (DEC_BATCH, DEC_SEQ, D_MODEL), 1.0),
        "cache_k": nrm(ks[2], (DEPTH, DEC_BATCH, win_buf, N_HEADS_A, HEAD_DIM), 1.0),
        "cache_v": nrm(ks[3], (DEPTH, DEC_BATCH, win_buf, N_HEADS_A, HEAD_DIM), DEEPNORM_BETA),
        "state_ret": nrm(ks[4], (DEPTH, DEC_BATCH, N_HEADS_B, DK_B, DV_B), 0.5),
        "p_prompt": nrm(ks[5], (DEPTH, BATCH, SEQ, PLE_DIM), 1.0),
        "p_sample": nrm(ks[6], (DEPTH, DEC_BATCH, DEC_SEQ, PLE_DIM), 1.0),
        "w_in": nrm(ks[7], (DEPTH, D_MODEL, IN_WIDTH), D_MODEL ** -0.5) * jnp.asarray(col_scale),
        "w_out": nrm(ks[8], (DEPTH, MIX_WIDTH, D_MODEL), MIX_WIDTH ** -0.5 * DEEPNORM_BETA),
        "gn_g": 1.0 + nrm(ks[9], (DEPTH, WIDTH_BV), 0.02),
        "gn_b": nrm(ks[10], (DEPTH, WIDTH_BV), 0.02),
        "ffn1_w1": nrm(ks[11], (DEPTH, D_MODEL, FFN_DIM), D_MODEL ** -0.5 * DEEPNORM_BETA),
        "ffn1_w3": nrm(ks[12], (DEPTH, D_MODEL, FFN_DIM), D_MODEL ** -0.5 * DEEPNORM_BETA),
        "ffn1_w2": nrm(ks[13], (DEPTH, FFN_DIM, D_MODEL), FFN_DIM ** -0.5 * DEEPNORM_BETA),
        "ffn2_w1": nrm(ks[14], (DEPTH, D_MODEL, FFN_DIM), D_MODEL ** -0.5 * DEEPNORM_BETA),
        "ffn2_w3": nrm(ks[15], (DEPTH, D_MODEL, FFN_DIM), D_MODEL ** -0.5 * DEEPNORM_BETA),
        "ffn2_w2": nrm(ks[16], (DEPTH, FFN_DIM, D_MODEL), FFN_DIM ** -0.5 * DEEPNORM_BETA),
        "w_ple": nrm(ks[17], (DEPTH, PLE_DIM, D_MODEL), PLE_DIM ** -0.5 * DEEPNORM_BETA),
        "w_gate": nrm(ks[18], (DEPTH, D_MODEL, D_MODEL), D_MODEL ** -0.5),
        "ln_g": 1.0 + nrm(ks[19], (DEPTH, 4, D_MODEL), 0.02),
        "ln_b": nrm(ks[20], (DEPTH, 4, D_MODEL), 0.02),
    }


def reference(x_prompt, x_sample, cache_k, cache_v, state_ret, p_prompt, p_sample,
              w_in, w_out, gn_g, gn_b, ffn1_w1, ffn1_w3, ffn1_w2, ffn2_w1, ffn2_w3, ffn2_w2,
              w_ple, w_gate, ln_g, ln_b):
    xp, xs = x_prompt, x_sample
    kp, vp, sp, ksm, vsm, ssm = [], [], [], [], [], []
    for i in range(DEPTH):
        layer_w = (ffn1_w1[i], ffn1_w3[i], ffn1_w2[i], ffn2_w1[i], ffn2_w3[i], ffn2_w2[i],
                   ln_g[i], ln_b[i], w_ple[i], w_gate[i])
        mix_p = functools.partial(mix_prompt, w_in=w_in[i], w_out=w_out[i],
                                  gn_g=gn_g[i], gn_b=gn_b[i])
        mix_s = functools.partial(mix_sample, cache_k=cache_k[i], cache_v=cache_v[i],
                                  state=state_ret[i], w_in=w_in[i], w_out=w_out[i],
                                  gn_g=gn_g[i], gn_b=gn_b[i])
        xp, (k_i, v_i, s_i) = decoder_layer(xp, p_prompt[i], mix_p, *layer_w)
        xs, (k_j, v_j, s_j) = decoder_layer(xs, p_sample[i], mix_s, *layer_w)
        kp.append(k_i); vp.append(v_i); sp.append(s_i)
        ksm.append(k_j); vsm.append(v_j); ssm.append(s_j)
    return (xp, xs, jnp.stack(kp), jnp.stack(vp), jnp.stack(sp),
            jnp.stack(ksm), jnp.stack(vsm), jnp.stack(ssm))
```

```python
import functools

import jax
import jax.numpy as jnp
from jax import lax
from jax.experimental import pallas as pl
from jax.experimental.pallas import tpu as pltpu

F32 = jnp.float32
BF16 = jnp.bfloat16

D_MODEL = 2048
DEPTH = 4
HEAD_DIM = 128
WIDTH_A = 1024
N_HEADS_A = 8
DILATIONS = ((128, 1), (512, 4), (2048, 16))
N_HEADS_B = 8
DV_B = 128
DK_B = 64
WIDTH_BQK = N_HEADS_B * DK_B
WIDTH_BV = N_HEADS_B * DV_B
IN_WIDTH = 3 * WIDTH_A + 2 * WIDTH_BQK + 2 * WIDTH_BV
FFN_DIM = 5632
PLE_DIM = 256
RET_CHUNK = 128
LN_EPS = 1e-5
GN_EPS = 1e-5
NEG_INF = -1e30
DEEPNORM_ALPHA = (2 * DEPTH) ** 0.25
ATTN_SCALE = HEAD_DIM ** -0.5
KB_SCALE = DK_B ** -0.5

BAND = 128
V7X_VMEM_LIMIT = 56 * 1024 * 1024
FFN_TILE_F = 512
LN_ROWS = 16


def _dot(a, b):
    return jnp.dot(a, b, preferred_element_type=F32)


def _dot_nt(a, b):
    return lax.dot_general(a, b, (((1,), (1,)), ((), ())), preferred_element_type=F32)


def _layer_norm_rows(y, g, b):
    mu = jnp.mean(y, axis=-1, keepdims=True)
    d = y - mu
    var = jnp.mean(d * d, axis=-1, keepdims=True)
    return d * lax.rsqrt(var + LN_EPS) * g + b


def _residual_ln_epilogue(x_ref, o_ref, g_ref, b_ref, branch_scale):
    rows = o_ref.shape[0]
    step = min(rows, LN_ROWS)
    g = g_ref[...]
    b = b_ref[...]

    def body(c, carry):
        r = pl.multiple_of(c * step, step)
        y = DEEPNORM_ALPHA * x_ref[pl.ds(r, step), :] + branch_scale * o_ref[pl.ds(r, step), :]
        o_ref[pl.ds(r, step), :] = _layer_norm_rows(y, g, b)
        return carry

    lax.fori_loop(0, rows // step, body, 0)


def _ffn_kernel(l_ref, x_ref, w1_ref, w3_ref, w2_ref, g_ref, b_ref, o_ref, xb_ref, *, nf):
    f = pl.program_id(1)

    @pl.when(f == 0)
    def _():
        xb_ref[...] = x_ref[...].astype(BF16)
        o_ref[...] = jnp.zeros_like(o_ref)

    xb = xb_ref[...]
    h1 = _dot(xb, w1_ref[...])
    h3 = _dot(xb, w3_ref[...])
    act = (h1 * jax.nn.sigmoid(h1) * h3).astype(BF16)
    o_ref[...] += _dot(act, w2_ref[...])

    @pl.when(f == nf - 1)
    def _():
        _residual_ln_epilogue(x_ref, o_ref, g_ref, b_ref, 0.5)


def _ffn_ln(layer, x, w1, w3, w2, ln_g, ln_b, ln_idx, tm):
    m = x.shape[0]
    nf = FFN_DIM // FFN_TILE_F
    return pl.pallas_call(
        functools.partial(_ffn_kernel, nf=nf),
        grid_spec=pltpu.PrefetchScalarGridSpec(
            num_scalar_prefetch=1,
            grid=(m // tm, nf),
            in_specs=[
                pl.BlockSpec((tm, D_MODEL), lambda i, f, l: (i, 0)),
                pl.BlockSpec((None, D_MODEL, FFN_TILE_F), lambda i, f, l: (l[0], 0, f)),
                pl.BlockSpec((None, D_MODEL, FFN_TILE_F), lambda i, f, l: (l[0], 0, f)),
                pl.BlockSpec((None, FFN_TILE_F, D_MODEL), lambda i, f, l: (l[0], f, 0)),
                pl.BlockSpec((None, None, 1, D_MODEL), lambda i, f, l: (l[0], ln_idx, 0, 0)),
                pl.BlockSpec((None, None, 1, D_MODEL), lambda i, f, l: (l[0], ln_idx, 0, 0)),
            ],
            out_specs=pl.BlockSpec((tm, D_MODEL), lambda i, f, l: (i, 0)),
            scratch_shapes=[pltpu.VMEM((tm, D_MODEL), BF16)],
        ),
        out_shape=jax.ShapeDtypeStruct((m, D_MODEL), F32),
        compiler_params=pltpu.CompilerParams(
            dimension_semantics=("parallel", "arbitrary"), vmem_limit_bytes=V7X_VMEM_LIMIT),
        name="ffn_ln",
    )(layer, x, w1, w3, w2, ln_g, ln_b)


def _proj_kernel(l_ref, x_ref, w_ref, o_ref, xb_ref):
    @pl.when(pl.program_id(1) == 0)
    def _():
        xb_ref[...] = x_ref[...].astype(BF16)

    o_ref[...] = _dot(xb_ref[...], w_ref[...])


def _in_proj(layer, x, w_in, tm, tn):
    m = x.shape[0]
    return pl.pallas_call(
        _proj_kernel,
        grid_spec=pltpu.PrefetchScalarGridSpec(
            num_scalar_prefetch=1,
            grid=(m // tm, IN_WIDTH // tn),
            in_specs=[
                pl.BlockSpec((tm, D_MODEL), lambda i, n, l: (i, 0)),
                pl.BlockSpec((None, D_MODEL, tn), lambda i, n, l: (l[0], 0, n)),
            ],
            out_specs=pl.BlockSpec((tm, tn), lambda i, n, l: (i, n)),
            scratch_shapes=[pltpu.VMEM((tm, D_MODEL), BF16)],
        ),
        out_shape=jax.ShapeDtypeStruct((m, IN_WIDTH), F32),
        compiler_params=pltpu.CompilerParams(
            dimension_semantics=("parallel", "arbitrary"), vmem_limit_bytes=V7X_VMEM_LIMIT),
        name="in_proj",
    )(layer, x, w_in)


def _out_proj_kernel(l_ref, oa_ref, ob_ref, x_ref, w_ref, g_ref, b_ref, o_ref):
    o_ref[...] = _dot(oa_ref[...], w_ref[0:WIDTH_A, :]) + _dot(ob_ref[...], w_ref[WIDTH_A:, :])
    _residual_ln_epilogue(x_ref, o_ref, g_ref, b_ref, 1.0)


def _out_proj_ln(layer, oa, ob, x, w_out, ln_g, ln_b, tm):
    m = x.shape[0]
    return pl.pallas_call(
        _out_proj_kernel,
        grid_spec=pltpu.PrefetchScalarGridSpec(
            num_scalar_prefetch=1,
            grid=(m // tm,),
            in_specs=[
                pl.BlockSpec((tm, WIDTH_A), lambda i, l: (i, 0)),
                pl.BlockSpec((tm, WIDTH_BV), lambda i, l: (i, 0)),
                pl.BlockSpec((tm, D_MODEL), lambda i, l: (i, 0)),
                pl.BlockSpec((None, WIDTH_A + WIDTH_BV, D_MODEL), lambda i, l: (l[0], 0, 0)),
                pl.BlockSpec((None, None, 1, D_MODEL), lambda i, l: (l[0], 1, 0, 0)),
                pl.BlockSpec((None, None, 1, D_MODEL), lambda i, l: (l[0], 1, 0, 0)),
            ],
            out_specs=pl.BlockSpec((tm, D_MODEL), lambda i, l: (i, 0)),
        ),
        out_shape=jax.ShapeDtypeStruct((m, D_MODEL), F32),
        compiler_params=pltpu.CompilerParams(
            dimension_semantics=("parallel",), vmem_limit_bytes=V7X_VMEM_LIMIT),
        name="out_proj_ln",
    )(layer, oa, ob, x, w_out, ln_g, ln_b)


def _ple_kernel(l_ref, x_ref, p_ref, wp_ref, wg_ref, g_ref, b_ref, o_ref):
    gate = jax.nn.sigmoid(_dot(x_ref[...].astype(BF16), wg_ref[...]))
    o_ref[...] = _dot(p_ref[...].astype(BF16), wp_ref[...]) * gate
    _residual_ln_epilogue(x_ref, o_ref, g_ref, b_ref, 1.0)


def _ple_ln(layer, x, p, w_ple, w_gate, ln_g, ln_b, tm):
    m = x.shape[0]
    return pl.pallas_call(
        _ple_kernel,
        grid_spec=pltpu.PrefetchScalarGridSpec(
            num_scalar_prefetch=1,
            grid=(m // tm,),
            in_specs=[
                pl.BlockSpec((tm, D_MODEL), lambda i, l: (i, 0)),
                pl.BlockSpec((None, tm, PLE_DIM), lambda i, l: (l[0], i, 0)),
                pl.BlockSpec((None, PLE_DIM, D_MODEL), lambda i, l: (l[0], 0, 0)),
                pl.BlockSpec((None, D_MODEL, D_MODEL), lambda i, l: (l[0], 0, 0)),
                pl.BlockSpec((None, None, 1, D_MODEL), lambda i, l: (l[0], 3, 0, 0)),
                pl.BlockSpec((None, None, 1, D_MODEL), lambda i, l: (l[0], 3, 0, 0)),
            ],
            out_specs=pl.BlockSpec((tm, D_MODEL), lambda i, l: (i, 0)),
        ),
        out_shape=jax.ShapeDtypeStruct((m, D_MODEL), F32),
        compiler_params=pltpu.CompilerParams(
            dimension_semantics=("parallel",), vmem_limit_bytes=V7X_VMEM_LIMIT),
        name="ple_ln",
    )(layer, x, p, w_ple, w_gate, ln_g, ln_b)


def _softmax_block(s, vs):
    m = jnp.max(s, axis=-1, keepdims=True)
    p = jnp.exp(s - m)
    l = jnp.sum(p, axis=-1, keepdims=True)
    o = _dot(p.astype(BF16), vs.astype(BF16)) / l
    return o, m + jnp.log(l)


def _attn_prompt_kernel(slope_ref, q_ref, k_ref, v_ref, o_ref, og_ref, lse_ref, *, seq):
    slope = slope_ref[pl.program_id(1)]
    qi1 = lax.broadcasted_iota(jnp.int32, (BAND, BAND), 0)
    ki1 = lax.broadcasted_iota(jnp.int32, (BAND, BAND), 1)
    dist1 = qi1 - ki1
    qi2 = lax.broadcasted_iota(jnp.int32, (BAND, 2 * BAND), 0)
    ki2 = lax.broadcasted_iota(jnp.int32, (BAND, 2 * BAND), 1)
    dist2 = qi2 + BAND - ki2
    valid1 = dist1 >= 0
    valid2 = (dist2 >= 0) & (dist2 <= BAND)

    def rows(start, size, dil):
        if dil == 1:
            return pl.ds(start, size)
        return pl.ds(start, size, stride=dil)

    for g, (_, dil) in enumerate(DILATIONS):
        nb = (seq // dil) // BAND
        bias1 = -(slope * dil) * dist1.astype(F32)
        bias2 = -(slope * dil) * dist2.astype(F32)

        def first_block(r, carry, dil=dil, g=g, bias1=bias1):
            qs = q_ref[rows(r, BAND, dil), :].astype(BF16)
            ks = k_ref[rows(r, BAND, dil), :].astype(BF16)
            vs = v_ref[rows(r, BAND, dil), :]
            s = _dot_nt(qs, ks) * ATTN_SCALE
            s = jnp.where(valid1, s + bias1, NEG_INF)
            o, lse = _softmax_block(s, vs)
            og_ref[g, rows(r, BAND, dil), :] = o
            lse_ref[g, rows(r, BAND, dil), :] = jnp.broadcast_to(lse, (BAND, HEAD_DIM))
            return carry

        lax.fori_loop(0, dil, first_block, 0)

        if nb > 1:
            def later_block(j, carry, dil=dil, g=g, nb=nb, bias2=bias2):
                r = j // (nb - 1)
                n = 1 + j % (nb - 1)
                q0 = n * (BAND * dil) + r
                k0 = q0 - BAND * dil
                qs = q_ref[rows(q0, BAND, dil), :].astype(BF16)
                ks = k_ref[rows(k0, 2 * BAND, dil), :].astype(BF16)
                vs = v_ref[rows(k0, 2 * BAND, dil), :]
                s = _dot_nt(qs, ks) * ATTN_SCALE
                s = jnp.where(valid2, s + bias2, NEG_INF)
                o, lse = _softmax_block(s, vs)
                og_ref[g, rows(q0, BAND, dil), :] = o
                lse_ref[g, rows(q0, BAND, dil), :] = jnp.broadcast_to(lse, (BAND, HEAD_DIM))
                return carry

            lax.fori_loop(0, dil * (nb - 1), later_block, 0)

    def merge(c, carry):
        r = pl.multiple_of(c * BAND, BAND)
        l0 = lse_ref[0, pl.ds(r, BAND), :]
        l1 = lse_ref[1, pl.ds(r, BAND), :]
        l2 = lse_ref[2, pl.ds(r, BAND), :]
        m = jnp.maximum(jnp.maximum(l0, l1), l2)
        e0 = jnp.exp(l0 - m)
        e1 = jnp.exp(l1 - m)
        e2 = jnp.exp(l2 - m)
        num = (e0 * og_ref[0, pl.ds(r, BAND), :] + e1 * og_ref[1, pl.ds(r, BAND), :]
               + e2 * og_ref[2, pl.ds(r, BAND), :])
        o_ref[pl.ds(r, BAND), :] = (num / (e0 + e1 + e2)).astype(BF16)
        return carry

    lax.fori_loop(0, seq // BAND, merge, 0)


def _attn_prompt(z, slopes, batch, seq):
    blk = lambda col0: pl.BlockSpec((seq, HEAD_DIM), lambda b, h, col0=col0: (b, col0 + h))
    return pl.pallas_call(
        functools.partial(_attn_prompt_kernel, seq=seq),
        grid=(batch, N_HEADS_A),
        in_specs=[
            pl.BlockSpec(memory_space=pltpu.SMEM),
            blk(0), blk(N_HEADS_A), blk(2 * N_HEADS_A),
        ],
        out_specs=pl.BlockSpec((seq, HEAD_DIM), lambda b, h: (b, h)),
        out_shape=jax.ShapeDtypeStruct((batch * seq, WIDTH_A), BF16),
        scratch_shapes=[pltpu.VMEM((len(DILATIONS), seq, HEAD_DIM), F32),
                        pltpu.VMEM((len(DILATIONS), seq, HEAD_DIM), F32)],
        compiler_params=pltpu.CompilerParams(
            dimension_semantics=("parallel", "parallel"), vmem_limit_bytes=V7X_VMEM_LIMIT),
        name="attn_prompt",
    )(slopes, z, z, z)


def _group_norm_gate(o, gate, gn_g, gn_b):
    mu = jnp.mean(o, axis=-1, keepdims=True)
    d = o - mu
    var = jnp.mean(d * d, axis=-1, keepdims=True)
    on = d * lax.rsqrt(var + GN_EPS)
    return (on * gn_g + gn_b) * (gate * jax.nn.sigmoid(gate))


def _ret_prompt_kernel(l_ref, lg_ref, q_ref, k_ref, v_ref, gate_ref, gng_ref, gnb_ref, o_ref,
                       s_ref, st_ref, *, seq):
    pair = pl.program_id(1)
    c_len = RET_CHUNK
    lane = lax.broadcasted_iota(jnp.int32, (c_len, 2 * DK_B), 1)
    pos_r = lax.broadcasted_iota(jnp.int32, (c_len, c_len), 0)
    pos_c = lax.broadcasted_iota(jnp.int32, (c_len, c_len), 1)
    rel = (pos_r - pos_c).astype(F32)
    pos = lax.broadcasted_iota(jnp.int32, (c_len, 1), 0).astype(F32)
    ones11 = jnp.ones((1, 1), F32)

    heads = []
    for hx in range(2):
        lg = lg_ref[2 * pair + hx]
        decay_mask = jnp.where(rel >= 0, jnp.exp(jnp.maximum(rel, 0.0) * lg), 0.0)
        q_decay = jnp.exp((pos + 1.0) * lg)
        k_decay = jnp.exp((c_len - 1.0 - pos) * lg)
        chunk_decay = jnp.exp((c_len * ones11) * lg)
        heads.append((decay_mask, q_decay, k_decay, chunk_decay))

    st_ref[...] = jnp.zeros_like(st_ref)

    def chunk(c, carry):
        r = pl.multiple_of(c * c_len, c_len)
        q2 = q_ref[pl.ds(r, c_len), :]
        k2 = k_ref[pl.ds(r, c_len), :] * KB_SCALE
        kb = k2.astype(BF16)
        for hx in range(2):
            decay_mask, q_decay, k_decay, chunk_decay = heads[hx]
            cols = slice(hx * DV_B, (hx + 1) * DV_B)
            qx = jnp.where((lane >= DK_B) if hx else (lane < DK_B), q2, 0.0).astype(BF16)
            vx = v_ref[pl.ds(r, c_len), cols].astype(BF16)
            state = st_ref[hx]
            inner = _dot_nt(qx, kb) * decay_mask
            o = _dot(inner.astype(BF16), vx) + _dot(qx, state.astype(BF16)) * q_decay
            kd_t = (k2 * k_decay).T.astype(BF16)
            st_ref[hx] = state * chunk_decay + _dot(kd_t, vx)
            res = _group_norm_gate(o, gate_ref[pl.ds(r, c_len), cols],
                                   gng_ref[:, cols], gnb_ref[:, cols])
            o_ref[pl.ds(r, c_len), cols] = res.astype(BF16)
        return carry

    lax.fori_loop(0, seq // c_len, chunk, 0)
    s_ref[0] = st_ref[0, 0:DK_B, :]
    s_ref[1] = st_ref[1, DK_B:2 * DK_B, :]


def _ret_prompt(layer, z, log_g, gn_g, gn_b, batch, seq):
    qcol = 3 * WIDTH_A // (2 * DK_B)
    kcol = qcol + WIDTH_BQK // (2 * DK_B)
    vcol = (3 * WIDTH_A + 2 * WIDTH_BQK) // (2 * DV_B)
    gcol = vcol + WIDTH_BV // (2 * DV_B)
    return pl.pallas_call(
        functools.partial(_ret_prompt_kernel, seq=seq),
        grid_spec=pltpu.PrefetchScalarGridSpec(
            num_scalar_prefetch=1,
            grid=(batch, N_HEADS_B // 2),
            in_specs=[
                pl.BlockSpec(memory_space=pltpu.SMEM),
                pl.BlockSpec((seq, 2 * DK_B), lambda b, p, l: (b, qcol + p)),
                pl.BlockSpec((seq, 2 * DK_B), lambda b, p, l: (b, kcol + p)),
                pl.BlockSpec((seq, 2 * DV_B), lambda b, p, l: (b, vcol + p)),
                pl.BlockSpec((seq, 2 * DV_B), lambda b, p, l: (b, gcol + p)),
                pl.BlockSpec((None, 1, 2 * DV_B), lambda b, p, l: (l[0], 0, p)),
                pl.BlockSpec((None, 1, 2 * DV_B), lambda b, p, l: (l[0], 0, p)),
            ],
            out_specs=[
                pl.BlockSpec((seq, 2 * DV_B), lambda b, p, l: (b, p)),
                pl.BlockSpec((None, 2, DK_B, DV_B), lambda b, p, l: (b, p, 0, 0)),
            ],
            scratch_shapes=[pltpu.VMEM((2, 2 * DK_B, DV_B), F32)],
        ),
        out_shape=[jax.ShapeDtypeStruct((batch * seq, WIDTH_BV), BF16),
                   jax.ShapeDtypeStruct((batch, N_HEADS_B, DK_B, DV_B), F32)],
        compiler_params=pltpu.CompilerParams(
            dimension_semantics=("parallel", "parallel"), vmem_limit_bytes=V7X_VMEM_LIMIT),
        name="ret_prompt",
    )(layer, log_g, z, z, z, z, gn_g, gn_b)


def _attn_decode_kernel(l_ref, slope_ref, z_ref, k1_ref, k4_ref, k16_ref, v1_ref, v4_ref, v16_ref,
                        o_ref):
    row = z_ref[0]
    steps = (BAND - lax.broadcasted_iota(jnp.int32, (BAND, 1), 0)).astype(F32)
    views = ((k1_ref, v1_ref, 1), (k4_ref, v4_ref, 4), (k16_ref, v16_ref, 16))
    for h in range(N_HEADS_A):
        cols = slice(h * HEAD_DIM, (h + 1) * HEAD_DIM)
        qh = row[:, cols]
        k_new = row[:, WIDTH_A + h * HEAD_DIM:WIDTH_A + (h + 1) * HEAD_DIM]
        v_new = row[:, 2 * WIDTH_A + h * HEAD_DIM:2 * WIDTH_A + (h + 1) * HEAD_DIM]
        slope = slope_ref[h]
        s_new = jnp.sum(qh * k_new, axis=1, keepdims=True) * ATTN_SCALE
        outs, lses = [], []
        for k_ref, v_ref, dil in views:
            s = (jnp.sum(k_ref[:, cols] * qh, axis=1, keepdims=True) * ATTN_SCALE
                 - slope * (steps * dil))
            m = jnp.maximum(jnp.max(s, axis=0, keepdims=True), s_new)
            e = jnp.exp(s - m)
            e_new = jnp.exp(s_new - m)
            l = jnp.sum(e, axis=0, keepdims=True) + e_new
            o = (jnp.sum(e * v_ref[:, cols], axis=0, keepdims=True) + e_new * v_new) / l
            outs.append(o)
            lses.append(m + jnp.log(l))
        m = jnp.maximum(jnp.maximum(lses[0], lses[1]), lses[2])
        w = [jnp.exp(x - m) for x in lses]
        o = (w[0] * outs[0] + w[1] * outs[1] + w[2] * outs[2]) / (w[0] + w[1] + w[2])
        o_ref[0, :, cols] = o.astype(BF16)


def _attn_decode(layer, slopes, z3, cache_k, cache_v):
    depth, nb, n_past = cache_k.shape[0], cache_k.shape[1], cache_k.shape[2]
    specs, args = [], []
    for cache in (cache_k, cache_v):
        for _, dil in DILATIONS:
            view = cache.reshape(depth, nb, n_past // dil, dil * WIDTH_A)
            last = n_past // dil // BAND - 1
            specs.append(pl.BlockSpec((None, None, BAND, WIDTH_A),
                                      lambda b, l, last=last: (l[0], b, last, 0)))
            args.append(view)
    return pl.pallas_call(
        _attn_decode_kernel,
        grid_spec=pltpu.PrefetchScalarGridSpec(
            num_scalar_prefetch=1,
            grid=(nb,),
            in_specs=[
                pl.BlockSpec(memory_space=pltpu.SMEM),
                pl.BlockSpec((1, 1, 3 * WIDTH_A), lambda b, l: (b, 0, 0)),
            ] + specs,
            out_specs=pl.BlockSpec((1, 1, WIDTH_A), lambda b, l: (b, 0, 0)),
        ),
        out_shape=jax.ShapeDtypeStruct((nb, 1, WIDTH_A), BF16),
        compiler_params=pltpu.CompilerParams(
            dimension_semantics=("parallel",), vmem_limit_bytes=V7X_VMEM_LIMIT),
        name="attn_decode",
    )(layer, slopes, z3, *args)


def _ret_decode_kernel(l_ref, lg_ref, z_ref, st_ref, gng_ref, gnb_ref, o_ref, sn_ref):
    row = z_ref[0]
    eye = (lax.broadcasted_iota(jnp.int32, (DK_B, DK_B), 0)
           == lax.broadcasted_iota(jnp.int32, (DK_B, DK_B), 1))
    ones11 = jnp.ones((1, 1), F32)
    v0 = 2 * WIDTH_BQK
    g0 = v0 + WIDTH_BV
    for h in range(N_HEADS_B):
        qh = row[:, h * DK_B:(h + 1) * DK_B]
        kh = row[:, WIDTH_BQK + h * DK_B:WIDTH_BQK + (h + 1) * DK_B] * KB_SCALE
        vh = row[:, v0 + h * DV_B:v0 + (h + 1) * DV_B]
        gh = row[:, g0 + h * DV_B:g0 + (h + 1) * DV_B]
        cols = slice(h * DV_B, (h + 1) * DV_B)
        gamma = jnp.exp(ones11 * lg_ref[h])
        q_col = jnp.sum(jnp.where(eye, qh, 0.0), axis=1, keepdims=True)
        k_col = jnp.sum(jnp.where(eye, kh, 0.0), axis=1, keepdims=True)
        state = st_ref[h]
        qk = jnp.sum(qh * kh, axis=1, keepdims=True)
        o = qk * vh + jnp.sum(q_col * state, axis=0, keepdims=True) * gamma
        sn_ref[h] = state * gamma + k_col * vh
        res = _group_norm_gate(o, gh, gng_ref[:, cols], gnb_ref[:, cols])
        o_ref[0, :, cols] = res.astype(BF16)


def _ret_decode(layer, log_g, z3, state_ret, gn_g, gn_b):
    nb = state_ret.shape[1]
    width = 2 * WIDTH_BQK + 2 * WIDTH_BV
    return pl.pallas_call(
        _ret_decode_kernel,
        grid_spec=pltpu.PrefetchScalarGridSpec(
            num_scalar_prefetch=1,
            grid=(nb,),
            in_specs=[
                pl.BlockSpec(memory_space=pltpu.SMEM),
                pl.BlockSpec((1, 1, width), lambda b, l: (b, 0, 1)),
                pl.BlockSpec((None, None, N_HEADS_B, DK_B, DV_B), lambda b, l: (l[0], b, 0, 0, 0)),
                pl.BlockSpec((None, 1, WIDTH_BV), lambda b, l: (l[0], 0, 0)),
                pl.BlockSpec((None, 1, WIDTH_BV), lambda b, l: (l[0], 0, 0)),
            ],
            out_specs=[
                pl.BlockSpec((1, 1, WIDTH_BV), lambda b, l: (b, 0, 0)),
                pl.BlockSpec((None, N_HEADS_B, DK_B, DV_B), lambda b, l: (b, 0, 0, 0)),
            ],
        ),
        out_shape=[jax.ShapeDtypeStruct((nb, 1, WIDTH_BV), BF16),
                   jax.ShapeDtypeStruct((nb, N_HEADS_B, DK_B, DV_B), F32)],
        compiler_params=pltpu.CompilerParams(
            dimension_semantics=("parallel",), vmem_limit_bytes=V7X_VMEM_LIMIT),
        name="ret_decode",
    )(layer, log_g, z3, state_ret, gn_g, gn_b)


def kernel(x_prompt, x_sample, cache_k, cache_v, state_ret, p_prompt, p_sample, w_in, w_out, gn_g,
           gn_b, ffn1_w1, ffn1_w3, ffn1_w2, ffn2_w1, ffn2_w3, ffn2_w2, w_ple, w_gate, ln_g, ln_b):
    batch, seq, _ = x_prompt.shape
    dec_batch, dec_seq, _ = x_sample.shape
    assert dec_seq == 1 and seq % (BAND * DILATIONS[-1][1]) == 0
    assert cache_k.shape[2] == BAND * DILATIONS[-1][1]
    m_p, m_s = batch * seq, dec_batch * dec_seq
    tm_p = 1024
    assert m_p % tm_p == 0

    bf = lambda w: w.astype(BF16)
    w_in_b, w_out_b, w_ple_b, w_gate_b = bf(w_in), bf(w_out), bf(w_ple), bf(w_gate)
    f1 = (bf(ffn1_w1), bf(ffn1_w3), bf(ffn1_w2))
    f2 = (bf(ffn2_w1), bf(ffn2_w3), bf(ffn2_w2))
    ln_g4 = ln_g.reshape(DEPTH, 4, 1, D_MODEL)
    ln_b4 = ln_b.reshape(DEPTH, 4, 1, D_MODEL)
    gn_g3 = gn_g.reshape(DEPTH, 1, WIDTH_BV)
    gn_b3 = gn_b.reshape(DEPTH, 1, WIDTH_BV)
    p_p = p_prompt.reshape(DEPTH, m_p, PLE_DIM)
    p_s = p_sample.reshape(DEPTH, m_s, PLE_DIM)

    heads_a = jnp.arange(1, N_HEADS_A + 1, dtype=F32)
    slopes = jnp.exp2(-8.0 * heads_a / N_HEADS_A)
    log_g = jnp.log(1.0 - jnp.exp2(-5.0 - jnp.arange(N_HEADS_B, dtype=F32)))

    def layer_step(carry, i):
        xp, xs = carry
        layer = i.reshape(1)

        xp = _ffn_ln(layer, xp, *f1, ln_g4, ln_b4, 0, tm_p)
        xs = _ffn_ln(layer, xs, *f1, ln_g4, ln_b4, 0, m_s)

        zp = _in_proj(layer, xp, w_in_b, tm_p, 512)
        zs = _in_proj(layer, xs, w_in_b, m_s, 512)

        oa_p = _attn_prompt(zp, slopes, batch, seq)
        ob_p, ret_p = _ret_prompt(layer, zp, log_g, gn_g3, gn_b3, batch, seq)
        zs3 = zs.reshape(m_s, 1, IN_WIDTH)
        oa_s = _attn_decode(layer, slopes, zs3, cache_k, cache_v).reshape(m_s, WIDTH_A)
        ob_s, ret_s = _ret_decode(layer, log_g, zs3, state_ret, gn_g3, gn_b3)
        ob_s = ob_s.reshape(m_s, WIDTH_BV)

        xp = _out_proj_ln(layer, oa_p, ob_p, xp, w_out_b, ln_g4, ln_b4, 512)
        xs = _out_proj_ln(layer, oa_s, ob_s, xs, w_out_b, ln_g4, ln_b4, m_s)

        xp = _ffn_ln(layer, xp, *f2, ln_g4, ln_b4, 2, tm_p)
        xs = _ffn_ln(layer, xs, *f2, ln_g4, ln_b4, 2, m_s)

        xp = _ple_ln(layer, xp, p_p, w_ple_b, w_gate_b, ln_g4, ln_b4, 512)
        xs = _ple_ln(layer, xs, p_s, w_ple_b, w_gate_b, ln_g4, ln_b4, m_s)

        k_p = zp[:, WIDTH_A:2 * WIDTH_A].reshape(batch, seq, N_HEADS_A, HEAD_DIM)
        v_p = zp[:, 2 * WIDTH_A:3 * WIDTH_A].reshape(batch, seq, N_HEADS_A, HEAD_DIM)
        k_s = zs[:, WIDTH_A:2 * WIDTH_A].reshape(dec_batch, dec_seq, N_HEADS_A, HEAD_DIM)
        v_s = zs[:, 2 * WIDTH_A:3 * WIDTH_A].reshape(dec_batch, dec_seq, N_HEADS_A, HEAD_DIM)
        return (xp, xs), (k_p, v_p, ret_p, k_s, v_s, ret_s)

    (xp, xs), (k_p, v_p, ret_p, k_s, v_s, ret_s) = lax.scan(
        layer_step, (x_prompt.reshape(m_p, D_MODEL), x_sample.reshape(m_s, D_MODEL)),
        jnp.arange(DEPTH, dtype=jnp.int32))
    return (xp.reshape(batch, seq, D_MODEL), xs.reshape(dec_batch, dec_seq, D_MODEL),
            k_p, v_p, ret_p, k_s, v_s, ret_s)
```

```python
import functools

import jax
import jax.numpy as jnp
from jax import lax
from jax.experimental import pallas as pl
from jax.experimental.pallas import tpu as pltpu

F32 = jnp.float32
BF16 = jnp.bfloat16

D_MODEL = 2048
DEPTH = 4
HEAD_DIM = 128
WIDTH_A = 1024
N_HEADS_A = 8
DILATIONS = ((128, 1), (512, 4), (2048, 16))
N_HEADS_B = 8
DV_B = 128
DK_B = 64
WIDTH_BQK = N_HEADS_B * DK_B
WIDTH_BV = N_HEADS_B * DV_B
IN_WIDTH = 3 * WIDTH_A + 2 * WIDTH_BQK + 2 * WIDTH_BV
FFN_DIM = 5632
PLE_DIM = 256
RET_CHUNK = 128
LN_EPS = 1e-5
GN_EPS = 1e-5
NEG_INF = -1e30
DEEPNORM_ALPHA = (2 * DEPTH) ** 0.25
ATTN_SCALE = HEAD_DIM ** -0.5
KB_SCALE = DK_B ** -0.5

BAND = 128
V7X_VMEM_LIMIT = 56 * 1024 * 1024
FFN_TILE_F = 512
LN_ROWS = 128
ATTN_UNROLL = 4


def _dot(a, b):
    return jnp.dot(a, b, preferred_element_type=F32)


def _dot_nt(a, b):
    return lax.dot_general(a, b, (((1,), (1,)), ((), ())), preferred_element_type=F32)


def _layer_norm_rows(y, g, b):
    mu = jnp.mean(y, axis=-1, keepdims=True)
    d = y - mu
    var = jnp.mean(d * d, axis=-1, keepdims=True)
    return d * lax.rsqrt(var + LN_EPS) * g + b


def _residual_ln_epilogue(x_ref, o_ref, g_ref, b_ref, branch_scale):
    rows = o_ref.shape[0]
    step = min(rows, LN_ROWS)
    g = g_ref[...]
    b = b_ref[...]

    def body(c, carry):
        r = pl.multiple_of(c * step, step)
        y = DEEPNORM_ALPHA * x_ref[pl.ds(r, step), :] + branch_scale * o_ref[pl.ds(r, step), :]
        o_ref[pl.ds(r, step), :] = _layer_norm_rows(y, g, b)
        return carry

    lax.fori_loop(0, rows // step, body, 0)


def _ffn_kernel(l_ref, x_ref, w1_ref, w3_ref, w2_ref, g_ref, b_ref, o_ref, xb_ref, *, nf):
    f = pl.program_id(1)

    @pl.when(f == 0)
    def _():
        xb_ref[...] = x_ref[...].astype(BF16)
        o_ref[...] = jnp.zeros_like(o_ref)

    xb = xb_ref[...]
    h1 = _dot(xb, w1_ref[...])
    h3 = _dot(xb, w3_ref[...])
    act = (h1 * jax.nn.sigmoid(h1) * h3).astype(BF16)
    o_ref[...] += _dot(act, w2_ref[...])

    @pl.when(f == nf - 1)
    def _():
        _residual_ln_epilogue(x_ref, o_ref, g_ref, b_ref, 0.5)


def _ffn_ln(layer, x, w1, w3, w2, ln_g, ln_b, ln_idx, tm):
    m = x.shape[0]
    nf = FFN_DIM // FFN_TILE_F
    return pl.pallas_call(
        functools.partial(_ffn_kernel, nf=nf),
        grid_spec=pltpu.PrefetchScalarGridSpec(
            num_scalar_prefetch=1,
            grid=(m // tm, nf),
            in_specs=[
                pl.BlockSpec((tm, D_MODEL), lambda i, f, l: (i, 0)),
                pl.BlockSpec((None, D_MODEL, FFN_TILE_F), lambda i, f, l: (l[0], 0, f)),
                pl.BlockSpec((None, D_MODEL, FFN_TILE_F), lambda i, f, l: (l[0], 0, f)),
                pl.BlockSpec((None, FFN_TILE_F, D_MODEL), lambda i, f, l: (l[0], f, 0)),
                pl.BlockSpec((None, None, 1, D_MODEL), lambda i, f, l: (l[0], ln_idx, 0, 0)),
                pl.BlockSpec((None, None, 1, D_MODEL), lambda i, f, l: (l[0], ln_idx, 0, 0)),
            ],
            out_specs=pl.BlockSpec((tm, D_MODEL), lambda i, f, l: (i, 0)),
            scratch_shapes=[pltpu.VMEM((tm, D_MODEL), BF16)],
        ),
        out_shape=jax.ShapeDtypeStruct((m, D_MODEL), F32),
        compiler_params=pltpu.CompilerParams(
            dimension_semantics=("parallel", "arbitrary"), vmem_limit_bytes=V7X_VMEM_LIMIT),
        name="ffn_ln",
    )(layer, x, w1, w3, w2, ln_g, ln_b)


def _proj_kernel(l_ref, x_ref, w_ref, o_ref, xb_ref):
    @pl.when(pl.program_id(1) == 0)
    def _():
        xb_ref[...] = x_ref[...].astype(BF16)

    o_ref[...] = _dot(xb_ref[...], w_ref[...])


def _in_proj(layer, x, w_in, tm, tn):
    m = x.shape[0]
    return pl.pallas_call(
        _proj_kernel,
        grid_spec=pltpu.PrefetchScalarGridSpec(
            num_scalar_prefetch=1,
            grid=(m // tm, IN_WIDTH // tn),
            in_specs=[
                pl.BlockSpec((tm, D_MODEL), lambda i, n, l: (i, 0)),
                pl.BlockSpec((None, D_MODEL, tn), lambda i, n, l: (l[0], 0, n)),
            ],
            out_specs=pl.BlockSpec((tm, tn), lambda i, n, l: (i, n)),
            scratch_shapes=[pltpu.VMEM((tm, D_MODEL), BF16)],
        ),
        out_shape=jax.ShapeDtypeStruct((m, IN_WIDTH), F32),
        compiler_params=pltpu.CompilerParams(
            dimension_semantics=("parallel", "arbitrary"), vmem_limit_bytes=V7X_VMEM_LIMIT),
        name="in_proj",
    )(layer, x, w_in)


def _out_proj_kernel(l_ref, oa_ref, ob_ref, x_ref, w_ref, g_ref, b_ref, o_ref):
    o_ref[...] = _dot(oa_ref[...], w_ref[0:WIDTH_A, :]) + _dot(ob_ref[...], w_ref[WIDTH_A:, :])
    _residual_ln_epilogue(x_ref, o_ref, g_ref, b_ref, 1.0)


def _out_proj_ln(layer, oa, ob, x, w_out, ln_g, ln_b, tm):
    m = x.shape[0]
    return pl.pallas_call(
        _out_proj_kernel,
        grid_spec=pltpu.PrefetchScalarGridSpec(
            num_scalar_prefetch=1,
            grid=(m // tm,),
            in_specs=[
                pl.BlockSpec((tm, WIDTH_A), lambda i, l: (i, 0)),
                pl.BlockSpec((tm, WIDTH_BV), lambda i, l: (i, 0)),
                pl.BlockSpec((tm, D_MODEL), lambda i, l: (i, 0)),
                pl.BlockSpec((None, WIDTH_A + WIDTH_BV, D_MODEL), lambda i, l: (l[0], 0, 0)),
                pl.BlockSpec((None, None, 1, D_MODEL), lambda i, l: (l[0], 1, 0, 0)),
                pl.BlockSpec((None, None, 1, D_MODEL), lambda i, l: (l[0], 1, 0, 0)),
            ],
            out_specs=pl.BlockSpec((tm, D_MODEL), lambda i, l: (i, 0)),
        ),
        out_shape=jax.ShapeDtypeStruct((m, D_MODEL), F32),
        compiler_params=pltpu.CompilerParams(
            dimension_semantics=("parallel",), vmem_limit_bytes=V7X_VMEM_LIMIT),
        name="out_proj_ln",
    )(layer, oa, ob, x, w_out, ln_g, ln_b)


def _ple_kernel(l_ref, x_ref, p_ref, wp_ref, wg_ref, g_ref, b_ref, o_ref):
    gate = jax.nn.sigmoid(_dot(x_ref[...].astype(BF16), wg_ref[...]))
    o_ref[...] = _dot(p_ref[...].astype(BF16), wp_ref[...]) * gate
    _residual_ln_epilogue(x_ref, o_ref, g_ref, b_ref, 1.0)


def _ple_ln(layer, x, p, w_ple, w_gate, ln_g, ln_b, tm):
    m = x.shape[0]
    return pl.pallas_call(
        _ple_kernel,
        grid_spec=pltpu.PrefetchScalarGridSpec(
            num_scalar_prefetch=1,
            grid=(m // tm,),
            in_specs=[
                pl.BlockSpec((tm, D_MODEL), lambda i, l: (i, 0)),
                pl.BlockSpec((None, tm, PLE_DIM), lambda i, l: (l[0], i, 0)),
                pl.BlockSpec((None, PLE_DIM, D_MODEL), lambda i, l: (l[0], 0, 0)),
                pl.BlockSpec((None, D_MODEL, D_MODEL), lambda i, l: (l[0], 0, 0)),
                pl.BlockSpec((None, None, 1, D_MODEL), lambda i, l: (l[0], 3, 0, 0)),
                pl.BlockSpec((None, None, 1, D_MODEL), lambda i, l: (l[0], 3, 0, 0)),
            ],
            out_specs=pl.BlockSpec((tm, D_MODEL), lambda i, l: (i, 0)),
        ),
        out_shape=jax.ShapeDtypeStruct((m, D_MODEL), F32),
        compiler_params=pltpu.CompilerParams(
            dimension_semantics=("parallel",), vmem_limit_bytes=V7X_VMEM_LIMIT),
        name="ple_ln",
    )(layer, x, p, w_ple, w_gate, ln_g, ln_b)


def _softmax_block(s, vs):
    m = jnp.max(s, axis=-1, keepdims=True)
    p = jnp.exp(s - m)
    l = jnp.sum(p, axis=-1, keepdims=True)
    o = _dot(p.astype(BF16), vs.astype(BF16)) / l
    return o, m + jnp.log(l)


def _attn_prompt_kernel(slope_ref, q_ref, k_ref, v_ref, o_ref, og_ref, lse_ref, *, seq):
    slope = slope_ref[pl.program_id(1)]
    qi1 = lax.broadcasted_iota(jnp.int32, (BAND, BAND), 0)
    ki1 = lax.broadcasted_iota(jnp.int32, (BAND, BAND), 1)
    dist1 = qi1 - ki1
    qi2 = lax.broadcasted_iota(jnp.int32, (BAND, 2 * BAND), 0)
    ki2 = lax.broadcasted_iota(jnp.int32, (BAND, 2 * BAND), 1)
    dist2 = qi2 + BAND - ki2
    valid1 = dist1 >= 0
    valid2 = (dist2 >= 0) & (dist2 <= BAND)

    def rows(start, size, dil):
        if dil == 1:
            return pl.ds(start, size)
        return pl.ds(start, size, stride=dil)

    for g, (_, dil) in enumerate(DILATIONS):
        nb = (seq // dil) // BAND
        bias1 = -(slope * dil) * dist1.astype(F32)
        bias2 = -(slope * dil) * dist2.astype(F32)

        def first_block(r, carry, dil=dil, g=g, bias1=bias1):
            qs = q_ref[rows(r, BAND, dil), :].astype(BF16)
            ks = k_ref[rows(r, BAND, dil), :].astype(BF16)
            vs = v_ref[rows(r, BAND, dil), :]
            s = _dot_nt(qs, ks) * ATTN_SCALE
            s = jnp.where(valid1, s + bias1, NEG_INF)
            o, lse = _softmax_block(s, vs)
            og_ref[g, rows(r, BAND, dil), :] = o
            lse_ref[g, rows(r, BAND, dil), :] = jnp.broadcast_to(lse, (BAND, HEAD_DIM))
            return carry

        lax.fori_loop(0, dil, first_block, 0, unroll=min(dil, ATTN_UNROLL))

        if nb > 1:
            def later_block(j, carry, dil=dil, g=g, nb=nb, bias2=bias2):
                r = j // (nb - 1)
                n = 1 + j % (nb - 1)
                q0 = n * (BAND * dil) + r
                k0 = q0 - BAND * dil
                qs = q_ref[rows(q0, BAND, dil), :].astype(BF16)
                ks = k_ref[rows(k0, 2 * BAND, dil), :].astype(BF16)
                vs = v_ref[rows(k0, 2 * BAND, dil), :]
                s = _dot_nt(qs, ks) * ATTN_SCALE
                s = jnp.where(valid2, s + bias2, NEG_INF)
                o, lse = _softmax_block(s, vs)
                og_ref[g, rows(q0, BAND, dil), :] = o
                lse_ref[g, rows(q0, BAND, dil), :] = jnp.broadcast_to(lse, (BAND, HEAD_DIM))
                return carry

            lax.fori_loop(0, dil * (nb - 1), later_block, 0, unroll=ATTN_UNROLL)

    def merge(c, carry):
        r = pl.multiple_of(c * BAND, BAND)
        l0 = lse_ref[0, pl.ds(r, BAND), :]
        l1 = lse_ref[1, pl.ds(r, BAND), :]
        l2 = lse_ref[2, pl.ds(r, BAND), :]
        m = jnp.maximum(jnp.maximum(l0, l1), l2)
        e0 = jnp.exp(l0 - m)
        e1 = jnp.exp(l1 - m)
        e2 = jnp.exp(l2 - m)
        num = (e0 * og_ref[0, pl.ds(r, BAND), :] + e1 * og_ref[1, pl.ds(r, BAND), :]
               + e2 * og_ref[2, pl.ds(r, BAND), :])
        o_ref[pl.ds(r, BAND), :] = (num / (e0 + e1 + e2)).astype(BF16)
        return carry

    lax.fori_loop(0, seq // BAND, merge, 0, unroll=2)


def _attn_prompt(z, slopes, batch, seq):
    blk = lambda col0: pl.BlockSpec((seq, HEAD_DIM), lambda b, h, col0=col0: (b, col0 + h))
    return pl.pallas_call(
        functools.partial(_attn_prompt_kernel, seq=seq),
        grid=(batch, N_HEADS_A),
        in_specs=[
            pl.BlockSpec(memory_space=pltpu.SMEM),
            blk(0), blk(N_HEADS_A), blk(2 * N_HEADS_A),
        ],
        out_specs=pl.BlockSpec((seq, HEAD_DIM), lambda b, h: (b, h)),
        out_shape=jax.ShapeDtypeStruct((batch * seq, WIDTH_A), BF16),
        scratch_shapes=[pltpu.VMEM((len(DILATIONS), seq, HEAD_DIM), F32),
                        pltpu.VMEM((len(DILATIONS), seq, HEAD_DIM), F32)],
        compiler_params=pltpu.CompilerParams(
            dimension_semantics=("parallel", "parallel"), vmem_limit_bytes=V7X_VMEM_LIMIT),
        name="attn_prompt",
    )(slopes, z, z, z)


def _group_norm_gate(o, gate, gn_g, gn_b):
    mu = jnp.mean(o, axis=-1, keepdims=True)
    d = o - mu
    var = jnp.mean(d * d, axis=-1, keepdims=True)
    on = d * lax.rsqrt(var + GN_EPS)
    return (on * gn_g + gn_b) * (gate * jax.nn.sigmoid(gate))


def _ret_prompt_kernel(l_ref, lg_ref, q_ref, k_ref, v_ref, gate_ref, gng_ref, gnb_ref, o_ref,
                       s_ref, st_ref, *, seq):
    pair = pl.program_id(1)
    c_len = RET_CHUNK
    lane = lax.broadcasted_iota(jnp.int32, (c_len, 2 * DK_B), 1)
    pos_r = lax.broadcasted_iota(jnp.int32, (c_len, c_len), 0)
    pos_c = lax.broadcasted_iota(jnp.int32, (c_len, c_len), 1)
    rel = (pos_r - pos_c).astype(F32)
    pos = lax.broadcasted_iota(jnp.int32, (c_len, 1), 0).astype(F32)
    ones11 = jnp.ones((1, 1), F32)

    heads = []
    for hx in range(2):
        lg = lg_ref[2 * pair + hx]
        decay_mask = jnp.where(rel >= 0, jnp.exp(jnp.maximum(rel, 0.0) * lg), 0.0)
        q_decay = jnp.exp((pos + 1.0) * lg)
        k_decay = jnp.exp((c_len - 1.0 - pos) * lg)
        chunk_decay = jnp.exp((c_len * ones11) * lg)
        heads.append((decay_mask, q_decay, k_decay, chunk_decay))

    st_ref[...] = jnp.zeros_like(st_ref)

    def chunk(c, carry):
        r = pl.multiple_of(c * c_len, c_len)
        q2 = q_ref[pl.ds(r, c_len), :]
        k2 = k_ref[pl.ds(r, c_len), :] * KB_SCALE
        kb = k2.astype(BF16)
        for hx in range(2):
            decay_mask, q_decay, k_decay, chunk_decay = heads[hx]
            cols = slice(hx * DV_B, (hx + 1) * DV_B)
            qx = jnp.where((lane >= DK_B) if hx else (lane < DK_B), q2, 0.0).astype(BF16)
            vx = v_ref[pl.ds(r, c_len), cols].astype(BF16)
            state = st_ref[hx]
            inner = _dot_nt(qx, kb) * decay_mask
            o = _dot(inner.astype(BF16), vx) + _dot(qx, state.astype(BF16)) * q_decay
            kd_t = (k2 * k_decay).T.astype(BF16)
            st_ref[hx] = state * chunk_decay + _dot(kd_t, vx)
            res = _group_norm_gate(o, gate_ref[pl.ds(r, c_len), cols],
                                   gng_ref[:, cols], gnb_ref[:, cols])
            o_ref[pl.ds(r, c_len), cols] = res.astype(BF16)
        return carry

    lax.fori_loop(0, seq // c_len, chunk, 0, unroll=2)
    s_ref[0] = st_ref[0, 0:DK_B, :]
    s_ref[1] = st_ref[1, DK_B:2 * DK_B, :]


def _ret_prompt(layer, z, log_g, gn_g, gn_b, batch, seq):
    qcol = 3 * WIDTH_A // (2 * DK_B)
    kcol = qcol + WIDTH_BQK // (2 * DK_B)
    vcol = (3 * WIDTH_A + 2 * WIDTH_BQK) // (2 * DV_B)
    gcol = vcol + WIDTH_BV // (2 * DV_B)
    return pl.pallas_call(
        functools.partial(_ret_prompt_kernel, seq=seq),
        grid_spec=pltpu.PrefetchScalarGridSpec(
            num_scalar_prefetch=1,
            grid=(batch, N_HEADS_B // 2),
            in_specs=[
                pl.BlockSpec(memory_space=pltpu.SMEM),
                pl.BlockSpec((seq, 2 * DK_B), lambda b, p, l: (b, qcol + p)),
                pl.BlockSpec((seq, 2 * DK_B), lambda b, p, l: (b, kcol + p)),
                pl.BlockSpec((seq, 2 * DV_B), lambda b, p, l: (b, vcol + p)),
                pl.BlockSpec((seq, 2 * DV_B), lambda b, p, l: (b, gcol + p)),
                pl.BlockSpec((None, 1, 2 * DV_B), lambda b, p, l: (l[0], 0, p)),
                pl.BlockSpec((None, 1, 2 * DV_B), lambda b, p, l: (l[0], 0, p)),
            ],
            out_specs=[
                pl.BlockSpec((seq, 2 * DV_B), lambda b, p, l: (b, p)),
                pl.BlockSpec((None, 2, DK_B, DV_B), lambda b, p, l: (b, p, 0, 0)),
            ],
            scratch_shapes=[pltpu.VMEM((2, 2 * DK_B, DV_B), F32)],
        ),
        out_shape=[jax.ShapeDtypeStruct((batch * seq, WIDTH_BV), BF16),
                   jax.ShapeDtypeStruct((batch, N_HEADS_B, DK_B, DV_B), F32)],
        compiler_params=pltpu.CompilerParams(
            dimension_semantics=("parallel", "parallel"), vmem_limit_bytes=V7X_VMEM_LIMIT),
        name="ret_prompt",
    )(layer, log_g, z, z, z, z, gn_g, gn_b)


def _attn_decode_kernel(l_ref, slope_ref, z_ref, k1_ref, k4_ref, k16_ref, v1_ref, v4_ref, v16_ref,
                        o_ref):
    row = z_ref[0]
    steps = (BAND - lax.broadcasted_iota(jnp.int32, (BAND, 1), 0)).astype(F32)
    views = ((k1_ref, v1_ref, 1), (k4_ref, v4_ref, 4), (k16_ref, v16_ref, 16))
    for h in range(N_HEADS_A):
        cols = slice(h * HEAD_DIM, (h + 1) * HEAD_DIM)
        qh = row[:, cols]
        k_new = row[:, WIDTH_A + h * HEAD_DIM:WIDTH_A + (h + 1) * HEAD_DIM]
        v_new = row[:, 2 * WIDTH_A + h * HEAD_DIM:2 * WIDTH_A + (h + 1) * HEAD_DIM]
        slope = slope_ref[h]
        s_new = jnp.sum(qh * k_new, axis=1, keepdims=True) * ATTN_SCALE
        outs, lses = [], []
        for k_ref, v_ref, dil in views:
            s = (jnp.sum(k_ref[:, h, :] * qh, axis=1, keepdims=True) * ATTN_SCALE
                 - slope * (steps * dil))
            m = jnp.maximum(jnp.max(s, axis=0, keepdims=True), s_new)
            e = jnp.exp(s - m)
            e_new = jnp.exp(s_new - m)
            l = jnp.sum(e, axis=0, keepdims=True) + e_new
            o = (jnp.sum(e * v_ref[:, h, :], axis=0, keepdims=True) + e_new * v_new) / l
            outs.append(o)
            lses.append(m + jnp.log(l))
        m = jnp.maximum(jnp.maximum(lses[0], lses[1]), lses[2])
        w = [jnp.exp(x - m) for x in lses]
        o = (w[0] * outs[0] + w[1] * outs[1] + w[2] * outs[2]) / (w[0] + w[1] + w[2])
        o_ref[0, :, cols] = o.astype(BF16)


def _attn_decode(layer, slopes, z3, cache_k, cache_v):
    depth, nb, n_past = cache_k.shape[0], cache_k.shape[1], cache_k.shape[2]
    specs, args = [], []
    for cache in (cache_k, cache_v):
        for _, dil in DILATIONS:
            view = cache.reshape(depth, nb, n_past // dil, dil, N_HEADS_A, HEAD_DIM)
            last = n_past // dil // BAND - 1
            specs.append(pl.BlockSpec((None, None, BAND, None, N_HEADS_A, HEAD_DIM),
                                      lambda b, l, last=last: (l[0], b, last, 0, 0, 0)))
            args.append(view)
    return pl.pallas_call(
        _attn_decode_kernel,
        grid_spec=pltpu.PrefetchScalarGridSpec(
            num_scalar_prefetch=1,
            grid=(nb,),
            in_specs=[
                pl.BlockSpec(memory_space=pltpu.SMEM),
                pl.BlockSpec((1, 1, 3 * WIDTH_A), lambda b, l: (b, 0, 0)),
            ] + specs,
            out_specs=pl.BlockSpec((1, 1, WIDTH_A), lambda b, l: (b, 0, 0)),
        ),
        out_shape=jax.ShapeDtypeStruct((nb, 1, WIDTH_A), BF16),
        compiler_params=pltpu.CompilerParams(
            dimension_semantics=("parallel",), vmem_limit_bytes=V7X_VMEM_LIMIT),
        name="attn_decode",
    )(layer, slopes, z3, *args)


def _ret_decode_kernel(l_ref, lg_ref, z_ref, st_ref, gng_ref, gnb_ref, o_ref, sn_ref):
    row = z_ref[0]
    eye = (lax.broadcasted_iota(jnp.int32, (DK_B, DK_B), 0)
           == lax.broadcasted_iota(jnp.int32, (DK_B, DK_B), 1))
    ones11 = jnp.ones((1, 1), F32)
    v0 = 2 * WIDTH_BQK
    g0 = v0 + WIDTH_BV
    for h in range(N_HEADS_B):
        qh = row[:, h * DK_B:(h + 1) * DK_B]
        kh = row[:, WIDTH_BQK + h * DK_B:WIDTH_BQK + (h + 1) * DK_B] * KB_SCALE
        vh = row[:, v0 + h * DV_B:v0 + (h + 1) * DV_B]
        gh = row[:, g0 + h * DV_B:g0 + (h + 1) * DV_B]
        cols = slice(h * DV_B, (h + 1) * DV_B)
        gamma = jnp.exp(ones11 * lg_ref[h])
        q_col = jnp.sum(jnp.where(eye, qh, 0.0), axis=1, keepdims=True)
        k_col = jnp.sum(jnp.where(eye, kh, 0.0), axis=1, keepdims=True)
        state = st_ref[h]
        qk = jnp.sum(qh * kh, axis=1, keepdims=True)
        o = qk * vh + jnp.sum(q_col * state, axis=0, keepdims=True) * gamma
        sn_ref[h] = state * gamma + k_col * vh
        res = _group_norm_gate(o, gh, gng_ref[:, cols], gnb_ref[:, cols])
        o_ref[0, :, cols] = res.astype(BF16)


def _ret_decode(layer, log_g, z3, state_ret, gn_g, gn_b):
    nb = state_ret.shape[1]
    width = 2 * WIDTH_BQK + 2 * WIDTH_BV
    return pl.pallas_call(
        _ret_decode_kernel,
        grid_spec=pltpu.PrefetchScalarGridSpec(
            num_scalar_prefetch=1,
            grid=(nb,),
            in_specs=[
                pl.BlockSpec(memory_space=pltpu.SMEM),
                pl.BlockSpec((1, 1, width), lambda b, l: (b, 0, 1)),
                pl.BlockSpec((None, None, N_HEADS_B, DK_B, DV_B), lambda b, l: (l[0], b, 0, 0, 0)),
                pl.BlockSpec((None, 1, WIDTH_BV), lambda b, l: (l[0], 0, 0)),
                pl.BlockSpec((None, 1, WIDTH_BV), lambda b, l: (l[0], 0, 0)),
            ],
            out_specs=[
                pl.BlockSpec((1, 1, WIDTH_BV), lambda b, l: (b, 0, 0)),
                pl.BlockSpec((None, N_HEADS_B, DK_B, DV_B), lambda b, l: (b, 0, 0, 0)),
            ],
        ),
        out_shape=[jax.ShapeDtypeStruct((nb, 1, WIDTH_BV), BF16),
                   jax.ShapeDtypeStruct((nb, N_HEADS_B, DK_B, DV_B), F32)],
        compiler_params=pltpu.CompilerParams(
            dimension_semantics=("parallel",), vmem_limit_bytes=V7X_VMEM_LIMIT),
        name="ret_decode",
    )(layer, log_g, z3, state_ret, gn_g, gn_b)


def kernel(x_prompt, x_sample, cache_k, cache_v, state_ret, p_prompt, p_sample, w_in, w_out, gn_g,
           gn_b, ffn1_w1, ffn1_w3, ffn1_w2, ffn2_w1, ffn2_w3, ffn2_w2, w_ple, w_gate, ln_g, ln_b):
    batch, seq, _ = x_prompt.shape
    dec_batch, dec_seq, _ = x_sample.shape
    assert dec_seq == 1 and seq % (BAND * DILATIONS[-1][1]) == 0
    assert cache_k.shape[2] == BAND * DILATIONS[-1][1]
    m_p, m_s = batch * seq, dec_batch * dec_seq
    tm_p = 1024
    assert m_p % tm_p == 0

    bf = lambda w: w.astype(BF16)
    w_in_b, w_out_b, w_ple_b, w_gate_b = bf(w_in), bf(w_out), bf(w_ple), bf(w_gate)
    f1 = (bf(ffn1_w1), bf(ffn1_w3), bf(ffn1_w2))
    f2 = (bf(ffn2_w1), bf(ffn2_w3), bf(ffn2_w2))
    ln_g4 = ln_g.reshape(DEPTH, 4, 1, D_MODEL)
    ln_b4 = ln_b.reshape(DEPTH, 4, 1, D_MODEL)
    gn_g3 = gn_g.reshape(DEPTH, 1, WIDTH_BV)
    gn_b3 = gn_b.reshape(DEPTH, 1, WIDTH_BV)
    p_p = p_prompt.reshape(DEPTH, m_p, PLE_DIM)
    p_s = p_sample.reshape(DEPTH, m_s, PLE_DIM)

    heads_a = jnp.arange(1, N_HEADS_A + 1, dtype=F32)
    slopes = jnp.exp2(-8.0 * heads_a / N_HEADS_A)
    log_g = jnp.log(1.0 - jnp.exp2(-5.0 - jnp.arange(N_HEADS_B, dtype=F32)))

    def layer_step(carry, i):
        xp, xs = carry
        layer = i.reshape(1)

        xp = _ffn_ln(layer, xp, *f1, ln_g4, ln_b4, 0, tm_p)
        xs = _ffn_ln(layer, xs, *f1, ln_g4, ln_b4, 0, m_s)

        zp = _in_proj(layer, xp, w_in_b, tm_p, 1024)
        zs = _in_proj(layer, xs, w_in_b, m_s, 1024)

        oa_p = _attn_prompt(zp, slopes, batch, seq)
        ob_p, ret_p = _ret_prompt(layer, zp, log_g, gn_g3, gn_b3, batch, seq)
        zs3 = zs.reshape(m_s, 1, IN_WIDTH)
        oa_s = _attn_decode(layer, slopes, zs3, cache_k, cache_v).reshape(m_s, WIDTH_A)
        ob_s, ret_s = _ret_decode(layer, log_g, zs3, state_ret, gn_g3, gn_b3)
        ob_s = ob_s.reshape(m_s, WIDTH_BV)

        xp = _out_proj_ln(layer, oa_p, ob_p, xp, w_out_b, ln_g4, ln_b4, 512)
        xs = _out_proj_ln(layer, oa_s, ob_s, xs, w_out_b, ln_g4, ln_b4, m_s)

        xp = _ffn_ln(layer, xp, *f2, ln_g4, ln_b4, 2, tm_p)
        xs = _ffn_ln(layer, xs, *f2, ln_g4, ln_b4, 2, m_s)

        xp = _ple_ln(layer, xp, p_p, w_ple_b, w_gate_b, ln_g4, ln_b4, 512)
        xs = _ple_ln(layer, xs, p_s, w_ple_b, w_gate_b, ln_g4, ln_b4, m_s)

        k_p = zp[:, WIDTH_A:2 * WIDTH_A].reshape(batch, seq, N_HEADS_A, HEAD_DIM)
        v_p = zp[:, 2 * WIDTH_A:3 * WIDTH_A].reshape(batch, seq, N_HEADS_A, HEAD_DIM)
        k_s = zs[:, WIDTH_A:2 * WIDTH_A].reshape(dec_batch, dec_seq, N_HEADS_A, HEAD_DIM)
        v_s = zs[:, 2 * WIDTH_A:3 * WIDTH_A].reshape(dec_batch, dec_seq, N_HEADS_A, HEAD_DIM)
        return (xp, xs), (k_p, v_p, ret_p, k_s, v_s, ret_s)

    (xp, xs), (k_p, v_p, ret_p, k_s, v_s, ret_s) = lax.scan(
        layer_step, (x_prompt.reshape(m_p, D_MODEL), x_sample.reshape(m_s, D_MODEL)),
        jnp.arange(DEPTH, dtype=jnp.int32))
    return (xp.reshape(batch, seq, D_MODEL), xs.reshape(dec_batch, dec_seq, D_MODEL),
            k_p, v_p, ret_p, k_s, v_s, ret_s)
```

```python
import functools

import jax
import jax.numpy as jnp
from jax import lax
from jax.experimental import pallas as pl
from jax.experimental.pallas import tpu as pltpu

F32 = jnp.float32
BF16 = jnp.bfloat16

D_MODEL = 2048
DEPTH = 4
HEAD_DIM = 128
WIDTH_A = 1024
N_HEADS_A = 8
DILATIONS = ((128, 1), (512, 4), (2048, 16))
N_HEADS_B = 8
DV_B = 128
DK_B = 64
WIDTH_BQK = N_HEADS_B * DK_B
WIDTH_BV = N_HEADS_B * DV_B
IN_WIDTH = 3 * WIDTH_A + 2 * WIDTH_BQK + 2 * WIDTH_BV
FFN_DIM = 5632
PLE_DIM = 256
RET_CHUNK = 128
LN_EPS = 1e-5
GN_EPS = 1e-5
NEG_INF = -1e30
DEEPNORM_ALPHA = (2 * DEPTH) ** 0.25
ATTN_SCALE = HEAD_DIM ** -0.5
KB_SCALE = DK_B ** -0.5

BAND = 128
V7X_VMEM_LIMIT = 56 * 1024 * 1024
V7X_VMEM_LIMIT_FFN = 62 * 1024 * 1024
FFN_TILE_F = 512
LN_ROWS = 128
ATTN_UNROLL = 4
ATTN_GROUP = 4
RET_BATCH = 2
LOG2E = 1.4426950408889634
LN2 = 0.6931471805599453


def _dot(a, b):
    return jnp.dot(a, b, preferred_element_type=F32)


def _dot_nt(a, b):
    return lax.dot_general(a, b, (((1,), (1,)), ((), ())), preferred_element_type=F32)


def _layer_norm_rows(y, g, b):
    mu = jnp.mean(y, axis=-1, keepdims=True)
    d = y - mu
    var = jnp.mean(d * d, axis=-1, keepdims=True)
    return d * lax.rsqrt(var + LN_EPS) * g + b


def _residual_ln_epilogue(x_ref, o_ref, g_ref, b_ref, branch_scale):
    rows = o_ref.shape[0]
    step = min(rows, LN_ROWS)
    g = g_ref[...]
    b = b_ref[...]

    def body(c, carry):
        r = pl.multiple_of(c * step, step)
        y = DEEPNORM_ALPHA * x_ref[pl.ds(r, step), :] + branch_scale * o_ref[pl.ds(r, step), :]
        o_ref[pl.ds(r, step), :] = _layer_norm_rows(y, g, b)
        return carry

    lax.fori_loop(0, rows // step, body, 0)


def _ffn_kernel(l_ref, x_ref, w1_ref, w3_ref, w2_ref, g_ref, b_ref, o_ref, xb_ref, *, nf):
    f = pl.program_id(1)

    @pl.when(f == 0)
    def _():
        xb_ref[...] = x_ref[...].astype(BF16)
        o_ref[...] = jnp.zeros_like(o_ref)

    xb = xb_ref[...]
    h1 = _dot(xb, w1_ref[...].astype(BF16))
    h3 = _dot(xb, w3_ref[...].astype(BF16))
    act = (h1 * jax.nn.sigmoid(h1) * h3).astype(BF16)
    o_ref[...] += _dot(act, w2_ref[...].astype(BF16))

    @pl.when(f == nf - 1)
    def _():
        _residual_ln_epilogue(x_ref, o_ref, g_ref, b_ref, 0.5)


def _ffn_ln(layer, x, w1, w3, w2, ln_g, ln_b, ln_idx, tm):
    m = x.shape[0]
    nf = FFN_DIM // FFN_TILE_F
    return pl.pallas_call(
        functools.partial(_ffn_kernel, nf=nf),
        grid_spec=pltpu.PrefetchScalarGridSpec(
            num_scalar_prefetch=1,
            grid=(m // tm, nf),
            in_specs=[
                pl.BlockSpec((tm, D_MODEL), lambda i, f, l: (i, 0), pipeline_mode=pl.Buffered(1)),
                pl.BlockSpec((None, D_MODEL, FFN_TILE_F), lambda i, f, l: (l[0], 0, f)),
                pl.BlockSpec((None, D_MODEL, FFN_TILE_F), lambda i, f, l: (l[0], 0, f)),
                pl.BlockSpec((None, FFN_TILE_F, D_MODEL), lambda i, f, l: (l[0], f, 0)),
                pl.BlockSpec((None, None, 1, D_MODEL), lambda i, f, l: (l[0], ln_idx, 0, 0)),
                pl.BlockSpec((None, None, 1, D_MODEL), lambda i, f, l: (l[0], ln_idx, 0, 0)),
            ],
            out_specs=pl.BlockSpec((tm, D_MODEL), lambda i, f, l: (i, 0),
                                   pipeline_mode=pl.Buffered(1)),
            scratch_shapes=[pltpu.VMEM((tm, D_MODEL), BF16)],
        ),
        out_shape=jax.ShapeDtypeStruct((m, D_MODEL), F32),
        compiler_params=pltpu.CompilerParams(
            dimension_semantics=("parallel", "arbitrary"), vmem_limit_bytes=V7X_VMEM_LIMIT_FFN),
        name="ffn_ln",
    )(layer, x, w1, w3, w2, ln_g, ln_b)


def _proj_kernel(l_ref, x_ref, w_ref, o_ref, xb_ref):
    @pl.when(pl.program_id(1) == 0)
    def _():
        xb_ref[...] = x_ref[...].astype(BF16)

    o_ref[...] = _dot(xb_ref[...], w_ref[...])


def _in_proj(layer, x, w_in, tm, tn, n_blocks, skip_after_first=0):
    m = x.shape[0]
    col = lambda n: n + jnp.where(n > 0, skip_after_first, 0)
    return pl.pallas_call(
        _proj_kernel,
        grid_spec=pltpu.PrefetchScalarGridSpec(
            num_scalar_prefetch=1,
            grid=(m // tm, n_blocks),
            in_specs=[
                pl.BlockSpec((tm, D_MODEL), lambda i, n, l: (i, 0)),
                pl.BlockSpec((None, D_MODEL, tn), lambda i, n, l: (l[0], 0, col(n))),
            ],
            out_specs=pl.BlockSpec((tm, tn), lambda i, n, l: (i, n)),
            scratch_shapes=[pltpu.VMEM((tm, D_MODEL), BF16)],
        ),
        out_shape=jax.ShapeDtypeStruct((m, n_blocks * tn), F32),
        compiler_params=pltpu.CompilerParams(
            dimension_semantics=("parallel", "arbitrary"), vmem_limit_bytes=V7X_VMEM_LIMIT),
        name="in_proj",
    )(layer, x, w_in)


def _proj_kv_kernel(l_ref, x_ref, w_ref, k_in_ref, v_in_ref, k_ref, v_ref, kv_ref, xb_ref):
    n = pl.program_id(1)

    @pl.when(n == 0)
    def _():
        xb_ref[...] = x_ref[...].astype(BF16)

    def store_heads(dst_ref):
        kv_ref[...] = _dot(xb_ref[...], w_ref[...])
        for h in range(N_HEADS_A):
            dst_ref[:, h, :] = kv_ref[:, h * HEAD_DIM:(h + 1) * HEAD_DIM]

    @pl.when(n == 0)
    def _():
        store_heads(k_ref)

    @pl.when(n == 1)
    def _():
        store_heads(v_ref)


def _in_proj_kv(layer, x, w_in, k_all, v_all, tm):
    m = x.shape[0]
    out_spec = pl.BlockSpec((None, tm, N_HEADS_A, HEAD_DIM), lambda i, n, l: (l[0], i, 0, 0))
    dense_spec = pl.BlockSpec((tm, WIDTH_A), lambda i, n, l: (i, n))
    return pl.pallas_call(
        _proj_kv_kernel,
        grid_spec=pltpu.PrefetchScalarGridSpec(
            num_scalar_prefetch=1,
            grid=(m // tm, 2),
            in_specs=[
                pl.BlockSpec((tm, D_MODEL), lambda i, n, l: (i, 0)),
                pl.BlockSpec((None, D_MODEL, WIDTH_A), lambda i, n, l: (l[0], 0, 1 + n)),
                pl.BlockSpec(memory_space=pl.ANY),
                pl.BlockSpec(memory_space=pl.ANY),
            ],
            out_specs=[out_spec, out_spec, dense_spec],
            scratch_shapes=[pltpu.VMEM((tm, D_MODEL), BF16)],
        ),
        out_shape=[jax.ShapeDtypeStruct(k_all.shape, F32), jax.ShapeDtypeStruct(v_all.shape, F32),
                   jax.ShapeDtypeStruct((m, 2 * WIDTH_A), F32)],
        input_output_aliases={3: 0, 4: 1},
        compiler_params=pltpu.CompilerParams(
            dimension_semantics=("parallel", "arbitrary"), vmem_limit_bytes=V7X_VMEM_LIMIT),
        name="in_proj_kv",
    )(layer, x, w_in, k_all, v_all)


def _out_proj_kernel(l_ref, oa_ref, ob_ref, x_ref, w_ref, g_ref, b_ref, o_ref):
    o_ref[...] = _dot(oa_ref[...], w_ref[0:WIDTH_A, :]) + _dot(ob_ref[...], w_ref[WIDTH_A:, :])
    _residual_ln_epilogue(x_ref, o_ref, g_ref, b_ref, 1.0)


def _out_proj_ln(layer, oa, ob, x, w_out, ln_g, ln_b, tm):
    m = x.shape[0]
    return pl.pallas_call(
        _out_proj_kernel,
        grid_spec=pltpu.PrefetchScalarGridSpec(
            num_scalar_prefetch=1,
            grid=(m // tm,),
            in_specs=[
                pl.BlockSpec((tm, WIDTH_A), lambda i, l: (i, 0)),
                pl.BlockSpec((tm, WIDTH_BV), lambda i, l: (i, 0)),
                pl.BlockSpec((tm, D_MODEL), lambda i, l: (i, 0)),
                pl.BlockSpec((None, WIDTH_A + WIDTH_BV, D_MODEL), lambda i, l: (l[0], 0, 0)),
                pl.BlockSpec((None, None, 1, D_MODEL), lambda i, l: (l[0], 1, 0, 0)),
                pl.BlockSpec((None, None, 1, D_MODEL), lambda i, l: (l[0], 1, 0, 0)),
            ],
            out_specs=pl.BlockSpec((tm, D_MODEL), lambda i, l: (i, 0)),
        ),
        out_shape=jax.ShapeDtypeStruct((m, D_MODEL), F32),
        compiler_params=pltpu.CompilerParams(
            dimension_semantics=("parallel",), vmem_limit_bytes=V7X_VMEM_LIMIT),
        name="out_proj_ln",
    )(layer, oa, ob, x, w_out, ln_g, ln_b)


def _ple_kernel(l_ref, x_ref, p_ref, wp_ref, wg_ref, g_ref, b_ref, o_ref):
    gate = jax.nn.sigmoid(_dot(x_ref[...].astype(BF16), wg_ref[...]))
    o_ref[...] = _dot(p_ref[...].astype(BF16), wp_ref[...]) * gate
    _residual_ln_epilogue(x_ref, o_ref, g_ref, b_ref, 1.0)


def _ple_ln(layer, x, p, w_ple, w_gate, ln_g, ln_b, tm):
    m = x.shape[0]
    return pl.pallas_call(
        _ple_kernel,
        grid_spec=pltpu.PrefetchScalarGridSpec(
            num_scalar_prefetch=1,
            grid=(m // tm,),
            in_specs=[
                pl.BlockSpec((tm, D_MODEL), lambda i, l: (i, 0)),
                pl.BlockSpec((None, tm, PLE_DIM), lambda i, l: (l[0], i, 0)),
                pl.BlockSpec((None, PLE_DIM, D_MODEL), lambda i, l: (l[0], 0, 0)),
                pl.BlockSpec((None, D_MODEL, D_MODEL), lambda i, l: (l[0], 0, 0)),
                pl.BlockSpec((None, None, 1, D_MODEL), lambda i, l: (l[0], 3, 0, 0)),
                pl.BlockSpec((None, None, 1, D_MODEL), lambda i, l: (l[0], 3, 0, 0)),
            ],
            out_specs=pl.BlockSpec((tm, D_MODEL), lambda i, l: (i, 0)),
        ),
        out_shape=jax.ShapeDtypeStruct((m, D_MODEL), F32),
        compiler_params=pltpu.CompilerParams(
            dimension_semantics=("parallel",), vmem_limit_bytes=V7X_VMEM_LIMIT),
        name="ple_ln",
    )(layer, x, p, w_ple, w_gate, ln_g, ln_b)


def _merge_pair(o_a, l_a, o_b, l_b):
    m = jnp.maximum(l_a, l_b)
    e_a = jnp.exp(l_a - m)
    e_b = jnp.exp(l_b - m)
    den = e_a + e_b
    return (e_a * o_a + e_b * o_b) / den, m + jnp.log(den)


def _attn_prompt_kernel(l_ref, slope_ref, q_ref, k_ref, v_ref, o_ref,
                        x4_ref, x16_ref, va_ref, s_ref, p_ref, ls_ref, bias_ref, og_ref, lse_ref,
                        *, seq):
    nblk = seq // BAND
    slab = ATTN_GROUP * BAND
    part4 = seq // 4
    slope = slope_ref[pl.program_id(1)]

    nat = (q_ref, k_ref, v_ref)
    for t in range(3):
        for a in range(4):
            for c in range(part4 // BAND):
                x4_ref[t, pl.ds(a * part4 + c * BAND, BAND), :] = (
                    nat[t][pl.ds(a + 4 * c * BAND, BAND, stride=4), :])
    for t in range(3):
        for a in range(4):
            for b in range(4):
                x16_ref[t, pl.ds((a + 4 * b) * BAND, BAND), :] = (
                    x4_ref[t, pl.ds(a * part4 + b, BAND, stride=4), :])

    ones = jnp.ones((slab, HEAD_DIM), BF16)
    for slot, src in enumerate((lambda r: v_ref[r, :], lambda r: x4_ref[2, r, :],
                                lambda r: x16_ref[2, r, :])):
        for c in range(seq // slab):
            r = pl.ds(c * slab, slab)
            va_ref[slot, r, 0:HEAD_DIM] = src(r).astype(BF16)
            va_ref[slot, r, HEAD_DIM:2 * HEAD_DIM] = ones

    row = lax.broadcasted_iota(jnp.int32, (slab, 2 * BAND), 0)
    col = lax.broadcasted_iota(jnp.int32, (slab, 2 * BAND), 1)
    dist = jnp.bitwise_and(row, BAND - 1) + BAND - col
    in_band = (dist >= 0) & (dist <= BAND)
    starts_class = (row < BAND) & (col < BAND)
    distf = dist.astype(F32)
    tile1 = jnp.where(in_band, -(slope * (1 * LOG2E)) * distf, NEG_INF)
    bias_ref[1] = tile1
    bias_ref[0] = jnp.where(starts_class, NEG_INF, tile1)
    bias_ref[2] = jnp.where(starts_class | ~in_band, NEG_INF, -(slope * (4 * LOG2E)) * distf)
    dist_own = dist[:, BAND:]
    bias_ref[3, :, 0:BAND] = jnp.where(dist_own >= 0,
                                       -(slope * (16 * LOG2E)) * dist_own.astype(F32), NEG_INF)

    def band_pass(qf, kf, vf, two, first_tile, rest_tile, store):
        kw = 2 * BAND if two else BAND

        if two:
            s_ref[0:BAND, 0:BAND] = jnp.zeros((BAND, BAND), F32)
            s_ref[0:BAND, BAND:kw] = _dot_nt(qf(0, BAND).astype(BF16), kf(0, BAND).astype(BF16))

            def scores(j, carry):
                r = pl.multiple_of(j * BAND, BAND)
                rp = pl.multiple_of(r - BAND, BAND)
                s_ref[pl.ds(r, BAND), :] = _dot_nt(qf(r, BAND).astype(BF16),
                                                   kf(rp, kw).astype(BF16))
                return carry

            lax.fori_loop(1, nblk, scores, 0, unroll=ATTN_UNROLL)
        else:
            def scores(j, carry):
                r = pl.multiple_of(j * BAND, BAND)
                s_ref[pl.ds(r, BAND), 0:kw] = _dot_nt(qf(r, BAND).astype(BF16),
                                                      kf(r, BAND).astype(BF16))
                return carry

            lax.fori_loop(0, nblk, scores, 0, unroll=ATTN_UNROLL)

        def softmax_slab(r, tile):
            s = s_ref[pl.ds(r, slab), 0:kw] * (ATTN_SCALE * LOG2E) + bias_ref[tile, :, 0:kw]
            m = jnp.max(s, axis=-1, keepdims=True)
            p = jnp.exp2(s - m)
            p_ref[pl.ds(r, slab), 0:kw] = p.astype(BF16)
            ls_ref[pl.ds(r, slab), :] = jnp.broadcast_to(m * LN2, (slab, HEAD_DIM))

        first_rest = 0
        if first_tile != rest_tile:
            softmax_slab(0, first_tile)
            first_rest = 1

        def rest_slab(g, carry):
            softmax_slab(pl.multiple_of(g * slab, slab), rest_tile)
            return carry

        lax.fori_loop(first_rest, nblk // ATTN_GROUP, rest_slab, 0)

        def finish(j, r, o_l):
            l = o_l[:, HEAD_DIM:]
            store(j, o_l[:, 0:HEAD_DIM] / l, ls_ref[pl.ds(r, BAND), :] + jnp.log(l))

        if two:
            finish(0, 0, _dot(p_ref[0:BAND, BAND:kw], vf(0, BAND)))

            def values(j, carry):
                r = pl.multiple_of(j * BAND, BAND)
                rp = pl.multiple_of(r - BAND, BAND)
                finish(j, r, _dot(p_ref[pl.ds(r, BAND), :], vf(rp, kw)))
                return carry

            lax.fori_loop(1, nblk, values, 0, unroll=ATTN_UNROLL)
        else:
            def values(j, carry):
                r = pl.multiple_of(j * BAND, BAND)
                finish(j, r, _dot(p_ref[pl.ds(r, BAND), 0:kw], vf(r, BAND)))
                return carry

            lax.fori_loop(0, nblk, values, 0, unroll=ATTN_UNROLL)

    def store_contiguous(slot):
        def store(j, o, lse):
            og_ref[slot, pl.ds(j * BAND, BAND), :] = o
            lse_ref[slot, pl.ds(j * BAND, BAND), :] = lse
        return store

    def store_16_as_4(j, o, lse):
        start = jnp.bitwise_and(j, 3) * part4 + jnp.right_shift(j, 2)
        og_ref[2, pl.ds(start, BAND, stride=4), :] = o
        lse_ref[2, pl.ds(start, BAND, stride=4), :] = lse

    values_of = lambda slot: (lambda start, size: va_ref[slot, pl.ds(start, size), :])
    natural = [lambda start, size, ref=ref: ref[pl.ds(start, size), :] for ref in nat[:2]]
    by4 = [lambda start, size, t=t: x4_ref[t, pl.ds(start, size), :] for t in range(2)]
    by16 = [lambda start, size, t=t: x16_ref[t, pl.ds(start, size), :] for t in range(2)]
    natural.append(values_of(0))
    by4.append(values_of(1))
    by16.append(values_of(2))
    band_pass(*natural, True, 0, 1, store_contiguous(0))
    band_pass(*by4, True, 2, 2, store_contiguous(1))
    band_pass(*by16, False, 3, 3, store_16_as_4)

    for a in range(4):
        for c in range(part4 // BAND):
            src = pl.ds(a * part4 + c * BAND, BAND)
            dst = pl.ds(a + 4 * c * BAND, BAND, stride=4)
            o, lse = _merge_pair(og_ref[1, src, :], lse_ref[1, src, :],
                                 og_ref[2, src, :], lse_ref[2, src, :])
            og_ref[3, dst, :] = o
            lse_ref[3, dst, :] = lse

    def final_merge(c, carry):
        rows = pl.ds(pl.multiple_of(c * BAND, BAND), BAND)
        o, _ = _merge_pair(og_ref[0, rows, :], lse_ref[0, rows, :],
                           og_ref[3, rows, :], lse_ref[3, rows, :])
        o_ref[rows, :] = o.astype(BF16)
        return carry

    lax.fori_loop(0, nblk, final_merge, 0, unroll=2)


def _attn_prompt(layer, zqb, kv, slopes, batch, seq):
    slab = ATTN_GROUP * BAND
    return pl.pallas_call(
        functools.partial(_attn_prompt_kernel, seq=seq),
        grid_spec=pltpu.PrefetchScalarGridSpec(
            num_scalar_prefetch=1,
            grid=(batch, N_HEADS_A),
            in_specs=[
                pl.BlockSpec(memory_space=pltpu.SMEM),
                pl.BlockSpec((seq, HEAD_DIM), lambda b, h, l: (b, h)),
                pl.BlockSpec((seq, HEAD_DIM), lambda b, h, l: (b, h)),
                pl.BlockSpec((seq, HEAD_DIM), lambda b, h, l: (b, N_HEADS_A + h)),
            ],
            out_specs=pl.BlockSpec((seq, HEAD_DIM), lambda b, h, l: (b, h)),
            scratch_shapes=[
                pltpu.VMEM((3, seq, HEAD_DIM), F32),
                pltpu.VMEM((3, seq, HEAD_DIM), F32),
                pltpu.VMEM((3, seq, 2 * HEAD_DIM), BF16),
                pltpu.VMEM((seq, 2 * BAND), F32),
                pltpu.VMEM((seq, 2 * BAND), BF16),
                pltpu.VMEM((seq, HEAD_DIM), F32),
                pltpu.VMEM((4, slab, 2 * BAND), F32),
                pltpu.VMEM((4, seq, HEAD_DIM), F32),
                pltpu.VMEM((4, seq, HEAD_DIM), F32),
            ],
        ),
        out_shape=jax.ShapeDtypeStruct((batch * seq, WIDTH_A), BF16),
        compiler_params=pltpu.CompilerParams(
            dimension_semantics=("parallel", "parallel"), vmem_limit_bytes=V7X_VMEM_LIMIT),
        name="attn_prompt",
    )(layer, slopes, zqb, kv, kv)


def _group_norm_gate(o, gate, gn_g, gn_b):
    mu = jnp.mean(o, axis=-1, keepdims=True)
    d = o - mu
    var = jnp.mean(d * d, axis=-1, keepdims=True)
    on = d * lax.rsqrt(var + GN_EPS)
    return (on * gn_g + gn_b) * (gate * jax.nn.sigmoid(gate))


def _ret_prompt_kernel(l_ref, lg_ref, q_ref, k_ref, v_ref, gate_ref, gng_ref, gnb_ref, o_ref,
                       s_ref, st_ref, *, seq):
    pair = pl.program_id(1)
    c_len = RET_CHUNK
    lane = lax.broadcasted_iota(jnp.int32, (c_len, 2 * DK_B), 1)
    pos_r = lax.broadcasted_iota(jnp.int32, (c_len, c_len), 0)
    pos_c = lax.broadcasted_iota(jnp.int32, (c_len, c_len), 1)
    rel = (pos_r - pos_c).astype(F32)
    pos = lax.broadcasted_iota(jnp.int32, (c_len, 1), 0).astype(F32)
    ones11 = jnp.ones((1, 1), F32)

    heads = []
    for hx in range(2):
        lg = lg_ref[2 * pair + hx]
        decay_mask = jnp.where(rel >= 0, jnp.exp(jnp.maximum(rel, 0.0) * lg), 0.0)
        q_decay = jnp.exp((pos + 1.0) * lg)
        k_decay = jnp.exp((c_len - 1.0 - pos) * lg)
        chunk_decay = jnp.exp((c_len * ones11) * lg)
        heads.append((decay_mask, q_decay, k_decay, chunk_decay))

    st_ref[...] = jnp.zeros_like(st_ref)

    def chunk(c, carry):
        for bi in range(RET_BATCH):
            r = pl.multiple_of(bi * seq + c * c_len, c_len)
            q2 = q_ref[pl.ds(r, c_len), :]
            k2 = k_ref[pl.ds(r, c_len), :] * KB_SCALE
            kb = k2.astype(BF16)
            for hx in range(2):
                decay_mask, q_decay, k_decay, chunk_decay = heads[hx]
                cols = slice(hx * DV_B, (hx + 1) * DV_B)
                qx = jnp.where((lane >= DK_B) if hx else (lane < DK_B), q2, 0.0).astype(BF16)
                vx = v_ref[pl.ds(r, c_len), cols].astype(BF16)
                state = st_ref[2 * bi + hx]
                inner = _dot_nt(qx, kb) * decay_mask
                o = _dot(inner.astype(BF16), vx) + _dot(qx, state.astype(BF16)) * q_decay
                kd_t = (k2 * k_decay).T.astype(BF16)
                st_ref[2 * bi + hx] = state * chunk_decay + _dot(kd_t, vx)
                res = _group_norm_gate(o, gate_ref[pl.ds(r, c_len), cols],
                                       gng_ref[:, cols], gnb_ref[:, cols])
                o_ref[pl.ds(r, c_len), cols] = res.astype(BF16)
        return carry

    lax.fori_loop(0, seq // c_len, chunk, 0)
    for bi in range(RET_BATCH):
        s_ref[bi, 0] = st_ref[2 * bi, 0:DK_B, :]
        s_ref[bi, 1] = st_ref[2 * bi + 1, DK_B:2 * DK_B, :]


def _ret_prompt(layer, z, log_g, gn_g, gn_b, batch, seq):
    assert batch % RET_BATCH == 0
    rows = RET_BATCH * seq
    qcol = WIDTH_A // (2 * DK_B)
    kcol = qcol + WIDTH_BQK // (2 * DK_B)
    vcol = (WIDTH_A + 2 * WIDTH_BQK) // (2 * DV_B)
    gcol = vcol + WIDTH_BV // (2 * DV_B)
    return pl.pallas_call(
        functools.partial(_ret_prompt_kernel, seq=seq),
        grid_spec=pltpu.PrefetchScalarGridSpec(
            num_scalar_prefetch=1,
            grid=(batch // RET_BATCH, N_HEADS_B // 2),
            in_specs=[
                pl.BlockSpec(memory_space=pltpu.SMEM),
                pl.BlockSpec((rows, 2 * DK_B), lambda b, p, l: (b, qcol + p)),
                pl.BlockSpec((rows, 2 * DK_B), lambda b, p, l: (b, kcol + p)),
                pl.BlockSpec((rows, 2 * DV_B), lambda b, p, l: (b, vcol + p)),
                pl.BlockSpec((rows, 2 * DV_B), lambda b, p, l: (b, gcol + p)),
                pl.BlockSpec((None, 1, 2 * DV_B), lambda b, p, l: (l[0], 0, p)),
                pl.BlockSpec((None, 1, 2 * DV_B), lambda b, p, l: (l[0], 0, p)),
            ],
            out_specs=[
                pl.BlockSpec((rows, 2 * DV_B), lambda b, p, l: (b, p)),
                pl.BlockSpec((RET_BATCH, 2, DK_B, DV_B), lambda b, p, l: (b, p, 0, 0)),
            ],
            scratch_shapes=[pltpu.VMEM((2 * RET_BATCH, 2 * DK_B, DV_B), F32)],
        ),
        out_shape=[jax.ShapeDtypeStruct((batch * seq, WIDTH_BV), BF16),
                   jax.ShapeDtypeStruct((batch, N_HEADS_B, DK_B, DV_B), F32)],
        compiler_params=pltpu.CompilerParams(
            dimension_semantics=("parallel", "parallel"), vmem_limit_bytes=V7X_VMEM_LIMIT),
        name="ret_prompt",
    )(layer, log_g, z, z, z, z, gn_g, gn_b)


def _attn_decode_kernel(l_ref, slope_ref, z_ref, k1_ref, k4_ref, k16_ref, v1_ref, v4_ref, v16_ref,
                        o_ref):
    row = z_ref[0]
    steps = (BAND - lax.broadcasted_iota(jnp.int32, (BAND, 1), 0)).astype(F32)
    views = ((k1_ref, v1_ref, 1), (k4_ref, v4_ref, 4), (k16_ref, v16_ref, 16))
    for h in range(N_HEADS_A):
        cols = slice(h * HEAD_DIM, (h + 1) * HEAD_DIM)
        qh = row[:, cols]
        k_new = row[:, WIDTH_A + h * HEAD_DIM:WIDTH_A + (h + 1) * HEAD_DIM]
        v_new = row[:, 2 * WIDTH_A + h * HEAD_DIM:2 * WIDTH_A + (h + 1) * HEAD_DIM]
        slope = slope_ref[h]
        s_new = jnp.sum(qh * k_new, axis=1, keepdims=True) * ATTN_SCALE
        outs, lses = [], []
        for k_ref, v_ref, dil in views:
            s = (jnp.sum(k_ref[:, h, :] * qh, axis=1, keepdims=True) * ATTN_SCALE
                 - slope * (steps * dil))
            m = jnp.maximum(jnp.max(s, axis=0, keepdims=True), s_new)
            e = jnp.exp(s - m)
            e_new = jnp.exp(s_new - m)
            l = jnp.sum(e, axis=0, keepdims=True) + e_new
            o = (jnp.sum(e * v_ref[:, h, :], axis=0, keepdims=True) + e_new * v_new) / l
            outs.append(o)
            lses.append(m + jnp.log(l))
        m = jnp.maximum(jnp.maximum(lses[0], lses[1]), lses[2])
        w = [jnp.exp(x - m) for x in lses]
        o = (w[0] * outs[0] + w[1] * outs[1] + w[2] * outs[2]) / (w[0] + w[1] + w[2])
        o_ref[0, :, cols] = o.astype(BF16)


def _attn_decode(layer, slopes, z3, cache_k, cache_v):
    depth, nb, n_past = cache_k.shape[0], cache_k.shape[1], cache_k.shape[2]
    specs, args = [], []
    for cache in (cache_k, cache_v):
        for _, dil in DILATIONS:
            view = cache.reshape(depth, nb, n_past // dil, dil, N_HEADS_A, HEAD_DIM)
            last = n_past // dil // BAND - 1
            specs.append(pl.BlockSpec((None, None, BAND, None, N_HEADS_A, HEAD_DIM),
                                      lambda b, l, last=last: (l[0], b, last, 0, 0, 0)))
            args.append(view)
    return pl.pallas_call(
        _attn_decode_kernel,
        grid_spec=pltpu.PrefetchScalarGridSpec(
            num_scalar_prefetch=1,
            grid=(nb,),
            in_specs=[
                pl.BlockSpec(memory_space=pltpu.SMEM),
                pl.BlockSpec((1, 1, 3 * WIDTH_A), lambda b, l: (b, 0, 0)),
            ] + specs,
            out_specs=pl.BlockSpec((1, 1, WIDTH_A), lambda b, l: (b, 0, 0)),
        ),
        out_shape=jax.ShapeDtypeStruct((nb, 1, WIDTH_A), BF16),
        compiler_params=pltpu.CompilerParams(
            dimension_semantics=("parallel",), vmem_limit_bytes=V7X_VMEM_LIMIT),
        name="attn_decode",
    )(layer, slopes, z3, *args)


def _ret_decode_kernel(l_ref, lg_ref, z_ref, st_ref, gng_ref, gnb_ref, o_ref, sn_ref):
    row = z_ref[0]
    eye = (lax.broadcasted_iota(jnp.int32, (DK_B, DK_B), 0)
           == lax.broadcasted_iota(jnp.int32, (DK_B, DK_B), 1))
    ones11 = jnp.ones((1, 1), F32)
    v0 = 2 * WIDTH_BQK
    g0 = v0 + WIDTH_BV
    for h in range(N_HEADS_B):
        qh = row[:, h * DK_B:(h + 1) * DK_B]
        kh = row[:, WIDTH_BQK + h * DK_B:WIDTH_BQK + (h + 1) * DK_B] * KB_SCALE
        vh = row[:, v0 + h * DV_B:v0 + (h + 1) * DV_B]
        gh = row[:, g0 + h * DV_B:g0 + (h + 1) * DV_B]
        cols = slice(h * DV_B, (h + 1) * DV_B)
        gamma = jnp.exp(ones11 * lg_ref[h])
        q_col = jnp.sum(jnp.where(eye, qh, 0.0), axis=1, keepdims=True)
        k_col = jnp.sum(jnp.where(eye, kh, 0.0), axis=1, keepdims=True)
        state = st_ref[h]
        qk = jnp.sum(qh * kh, axis=1, keepdims=True)
        o = qk * vh + jnp.sum(q_col * state, axis=0, keepdims=True) * gamma
        sn_ref[h] = state * gamma + k_col * vh
        res = _group_norm_gate(o, gh, gng_ref[:, cols], gnb_ref[:, cols])
        o_ref[0, :, cols] = res.astype(BF16)


def _ret_decode(layer, log_g, z3, state_ret, gn_g, gn_b):
    nb = state_ret.shape[1]
    width = 2 * WIDTH_BQK + 2 * WIDTH_BV
    return pl.pallas_call(
        _ret_decode_kernel,
        grid_spec=pltpu.PrefetchScalarGridSpec(
            num_scalar_prefetch=1,
            grid=(nb,),
            in_specs=[
                pl.BlockSpec(memory_space=pltpu.SMEM),
                pl.BlockSpec((1, 1, width), lambda b, l: (b, 0, 1)),
                pl.BlockSpec((None, None, N_HEADS_B, DK_B, DV_B), lambda b, l: (l[0], b, 0, 0, 0)),
                pl.BlockSpec((None, 1, WIDTH_BV), lambda b, l: (l[0], 0, 0)),
                pl.BlockSpec((None, 1, WIDTH_BV), lambda b, l: (l[0], 0, 0)),
            ],
            out_specs=[
                pl.BlockSpec((1, 1, WIDTH_BV), lambda b, l: (b, 0, 0)),
                pl.BlockSpec((None, N_HEADS_B, DK_B, DV_B), lambda b, l: (b, 0, 0, 0)),
            ],
        ),
        out_shape=[jax.ShapeDtypeStruct((nb, 1, WIDTH_BV), BF16),
                   jax.ShapeDtypeStruct((nb, N_HEADS_B, DK_B, DV_B), F32)],
        compiler_params=pltpu.CompilerParams(
            dimension_semantics=("parallel",), vmem_limit_bytes=V7X_VMEM_LIMIT),
        name="ret_decode",
    )(layer, log_g, z3, state_ret, gn_g, gn_b)


def kernel(x_prompt, x_sample, cache_k, cache_v, state_ret, p_prompt, p_sample, w_in, w_out, gn_g,
           gn_b, ffn1_w1, ffn1_w3, ffn1_w2, ffn2_w1, ffn2_w3, ffn2_w2, w_ple, w_gate, ln_g, ln_b):
    batch, seq, _ = x_prompt.shape
    dec_batch, dec_seq, _ = x_sample.shape
    assert dec_seq == 1 and seq == BAND * DILATIONS[-1][1]
    assert cache_k.shape[2] == BAND * DILATIONS[-1][1]
    m_p, m_s = batch * seq, dec_batch * dec_seq
    tm_p = 1024
    assert m_p % tm_p == 0

    bf = lambda w: w.astype(BF16)
    w_in_b, w_out_b, w_ple_b, w_gate_b = bf(w_in), bf(w_out), bf(w_ple), bf(w_gate)
    f1 = (ffn1_w1, ffn1_w3, ffn1_w2)
    f2 = (ffn2_w1, ffn2_w3, ffn2_w2)
    ln_g4 = ln_g.reshape(DEPTH, 4, 1, D_MODEL)
    ln_b4 = ln_b.reshape(DEPTH, 4, 1, D_MODEL)
    gn_g3 = gn_g.reshape(DEPTH, 1, WIDTH_BV)
    gn_b3 = gn_b.reshape(DEPTH, 1, WIDTH_BV)
    p_p = p_prompt.reshape(DEPTH, m_p, PLE_DIM)
    p_s = p_sample.reshape(DEPTH, m_s, PLE_DIM)

    heads_a = jnp.arange(1, N_HEADS_A + 1, dtype=F32)
    slopes = jnp.exp2(-8.0 * heads_a / N_HEADS_A)
    log_g = jnp.log(1.0 - jnp.exp2(-5.0 - jnp.arange(N_HEADS_B, dtype=F32)))

    def layer_step(carry, i):
        xp, xs, k_p, v_p = carry
        layer = i.reshape(1)

        xp = _ffn_ln(layer, xp, *f1, ln_g4, ln_b4, 0, tm_p)
        xs = _ffn_ln(layer, xs, *f1, ln_g4, ln_b4, 0, m_s)

        zp = _in_proj(layer, xp, w_in_b, tm_p, WIDTH_A, 4, skip_after_first=2)
        k_p, v_p, kv_p = _in_proj_kv(layer, xp, w_in_b, k_p, v_p, 512)
        zs = _in_proj(layer, xs, w_in_b, m_s, WIDTH_A, IN_WIDTH // WIDTH_A)

        oa_p = _attn_prompt(layer, zp, kv_p, slopes, batch, seq)
        ob_p, ret_p = _ret_prompt(layer, zp, log_g, gn_g3, gn_b3, batch, seq)
        zs3 = zs.reshape(m_s, 1, IN_WIDTH)
        oa_s = _attn_decode(layer, slopes, zs3, cache_k, cache_v).reshape(m_s, WIDTH_A)
        ob_s, ret_s = _ret_decode(layer, log_g, zs3, state_ret, gn_g3, gn_b3)
        ob_s = ob_s.reshape(m_s, WIDTH_BV)

        xp = _out_proj_ln(layer, oa_p, ob_p, xp, w_out_b, ln_g4, ln_b4, 512)
        xs = _out_proj_ln(layer, oa_s, ob_s, xs, w_out_b, ln_g4, ln_b4, m_s)

        xp = _ffn_ln(layer, xp, *f2, ln_g4, ln_b4, 2, tm_p)
        xs = _ffn_ln(layer, xs, *f2, ln_g4, ln_b4, 2, m_s)

        xp = _ple_ln(layer, xp, p_p, w_ple_b, w_gate_b, ln_g4, ln_b4, 512)
        xs = _ple_ln(layer, xs, p_s, w_ple_b, w_gate_b, ln_g4, ln_b4, m_s)

        k_s = zs[:, WIDTH_A:2 * WIDTH_A].reshape(dec_batch, dec_seq, N_HEADS_A, HEAD_DIM)
        v_s = zs[:, 2 * WIDTH_A:3 * WIDTH_A].reshape(dec_batch, dec_seq, N_HEADS_A, HEAD_DIM)
        return (xp, xs, k_p, v_p), (ret_p, k_s, v_s, ret_s)

    kv_shape = (DEPTH, m_p, N_HEADS_A, HEAD_DIM)
    (xp, xs, k_p, v_p), (ret_p, k_s, v_s, ret_s) = lax.scan(
        layer_step,
        (x_prompt.reshape(m_p, D_MODEL), x_sample.reshape(m_s, D_MODEL),
         jnp.zeros(kv_shape, F32), jnp.zeros(kv_shape, F32)),
        jnp.arange(DEPTH, dtype=jnp.int32))
    kv_out = (DEPTH, batch, seq, N_HEADS_A, HEAD_DIM)
    return (xp.reshape(batch, seq, D_MODEL), xs.reshape(dec_batch, dec_seq, D_MODEL),
            k_p.reshape(kv_out), v_p.reshape(kv_out), ret_p, k_s, v_s, ret_s)
```

```python
import functools

import jax
import jax.numpy as jnp
from jax import lax
from jax.experimental import pallas as pl
from jax.experimental.pallas import tpu as pltpu

F32 = jnp.float32
BF16 = jnp.bfloat16

D_MODEL = 2048
DEPTH = 4
HEAD_DIM = 128
WIDTH_A = 1024
N_HEADS_A = 8
DILATIONS = ((128, 1), (512, 4), (2048, 16))
N_HEADS_B = 8
DV_B = 128
DK_B = 64
WIDTH_BQK = N_HEADS_B * DK_B
WIDTH_BV = N_HEADS_B * DV_B
IN_WIDTH = 3 * WIDTH_A + 2 * WIDTH_BQK + 2 * WIDTH_BV
FFN_DIM = 5632
PLE_DIM = 256
RET_CHUNK = 128
LN_EPS = 1e-5
GN_EPS = 1e-5
NEG_INF = -1e30
DEEPNORM_ALPHA = (2 * DEPTH) ** 0.25
ATTN_SCALE = HEAD_DIM ** -0.5
KB_SCALE = DK_B ** -0.5

BAND = 128
V7X_VMEM_LIMIT = 56 * 1024 * 1024
V7X_VMEM_LIMIT_FFN = 62 * 1024 * 1024
FFN_TILE_F = 512
LN_ROWS = 128
ATTN_UNROLL = 4
ATTN_GROUP = 4
RET_BATCH = 2
LOG2E = 1.4426950408889634
LN2 = 0.6931471805599453


def _dot(a, b):
    return jnp.dot(a, b, preferred_element_type=F32)


def _dot_nt(a, b):
    return lax.dot_general(a, b, (((1,), (1,)), ((), ())), preferred_element_type=F32)


def _layer_norm_rows(y, g, b):
    mu = jnp.mean(y, axis=-1, keepdims=True)
    d = y - mu
    var = jnp.mean(d * d, axis=-1, keepdims=True)
    return d * lax.rsqrt(var + LN_EPS) * g + b


def _residual_ln_epilogue(x_ref, o_ref, g_ref, b_ref, branch_scale):
    rows = o_ref.shape[0]
    step = min(rows, LN_ROWS)
    g = g_ref[...]
    b = b_ref[...]

    def body(c, carry):
        r = pl.multiple_of(c * step, step)
        y = DEEPNORM_ALPHA * x_ref[pl.ds(r, step), :] + branch_scale * o_ref[pl.ds(r, step), :]
        o_ref[pl.ds(r, step), :] = _layer_norm_rows(y, g, b)
        return carry

    lax.fori_loop(0, rows // step, body, 0)


def _swiglu_step(xb, w1b, w3b, w2b):
    h1 = _dot(xb, w1b)
    h3 = _dot(xb, w3b)
    act = (h1 * jax.nn.sigmoid(h1) * h3).astype(BF16)
    return _dot(act, w2b)


def _ffn_kernel(l_ref, x_ref, xs_ref, w1_ref, w3_ref, w2_ref, g_ref, b_ref, o_ref, os_ref,
                xb_ref, xsb_ref, *, nf):
    i = pl.program_id(0)
    f = pl.program_id(1)

    @pl.when(f == 0)
    def _():
        xb_ref[...] = x_ref[...].astype(BF16)
        o_ref[...] = jnp.zeros_like(o_ref)

    @pl.when((f == 0) & (i == 0))
    def _():
        xsb_ref[...] = xs_ref[...].astype(BF16)
        os_ref[...] = jnp.zeros_like(os_ref)

    w1b = w1_ref[...].astype(BF16)
    w3b = w3_ref[...].astype(BF16)
    w2b = w2_ref[...].astype(BF16)
    o_ref[...] += _swiglu_step(xb_ref[...], w1b, w3b, w2b)

    @pl.when(i == 0)
    def _():
        os_ref[...] += _swiglu_step(xsb_ref[...], w1b, w3b, w2b)

    @pl.when(f == nf - 1)
    def _():
        _residual_ln_epilogue(x_ref, o_ref, g_ref, b_ref, 0.5)

    @pl.when((f == nf - 1) & (i == 0))
    def _():
        _residual_ln_epilogue(xs_ref, os_ref, g_ref, b_ref, 0.5)


def _ffn_ln(layer, x, xs, w1, w3, w2, ln_g, ln_b, ln_idx, tm):
    m, ms = x.shape[0], xs.shape[0]
    nf = FFN_DIM // FFN_TILE_F
    return pl.pallas_call(
        functools.partial(_ffn_kernel, nf=nf),
        grid_spec=pltpu.PrefetchScalarGridSpec(
            num_scalar_prefetch=1,
            grid=(m // tm, nf),
            in_specs=[
                pl.BlockSpec((tm, D_MODEL), lambda i, f, l: (i, 0), pipeline_mode=pl.Buffered(1)),
                pl.BlockSpec((ms, D_MODEL), lambda i, f, l: (0, 0)),
                pl.BlockSpec((None, D_MODEL, FFN_TILE_F), lambda i, f, l: (l[0], 0, f)),
                pl.BlockSpec((None, D_MODEL, FFN_TILE_F), lambda i, f, l: (l[0], 0, f)),
                pl.BlockSpec((None, FFN_TILE_F, D_MODEL), lambda i, f, l: (l[0], f, 0)),
                pl.BlockSpec((None, None, 1, D_MODEL), lambda i, f, l: (l[0], ln_idx, 0, 0)),
                pl.BlockSpec((None, None, 1, D_MODEL), lambda i, f, l: (l[0], ln_idx, 0, 0)),
            ],
            out_specs=[
                pl.BlockSpec((tm, D_MODEL), lambda i, f, l: (i, 0)),
                pl.BlockSpec((ms, D_MODEL), lambda i, f, l: (0, 0)),
            ],
            scratch_shapes=[pltpu.VMEM((tm, D_MODEL), BF16), pltpu.VMEM((ms, D_MODEL), BF16)],
        ),
        out_shape=[jax.ShapeDtypeStruct((m, D_MODEL), F32),
                   jax.ShapeDtypeStruct((ms, D_MODEL), F32)],
        compiler_params=pltpu.CompilerParams(
            dimension_semantics=("arbitrary", "arbitrary"), vmem_limit_bytes=V7X_VMEM_LIMIT_FFN),
        name="ffn_ln",
    )(layer, x, xs, w1, w3, w2, ln_g, ln_b)


def _proj_kernel(l_ref, x_ref, w_ref, o_ref, xb_ref):
    @pl.when(pl.program_id(1) == 0)
    def _():
        xb_ref[...] = x_ref[...].astype(BF16)

    o_ref[...] = _dot(xb_ref[...], w_ref[...].astype(BF16))


def _in_proj(layer, x, w_in, tm, tn, n_blocks, skip_after_first=0):
    m = x.shape[0]
    col = lambda n: n + jnp.where(n > 0, skip_after_first, 0)
    return pl.pallas_call(
        _proj_kernel,
        grid_spec=pltpu.PrefetchScalarGridSpec(
            num_scalar_prefetch=1,
            grid=(m // tm, n_blocks),
            in_specs=[
                pl.BlockSpec((tm, D_MODEL), lambda i, n, l: (i, 0)),
                pl.BlockSpec((None, D_MODEL, tn), lambda i, n, l: (l[0], 0, col(n))),
            ],
            out_specs=pl.BlockSpec((tm, tn), lambda i, n, l: (i, n)),
            scratch_shapes=[pltpu.VMEM((tm, D_MODEL), BF16)],
        ),
        out_shape=jax.ShapeDtypeStruct((m, n_blocks * tn), F32),
        compiler_params=pltpu.CompilerParams(
            dimension_semantics=("parallel", "arbitrary"), vmem_limit_bytes=V7X_VMEM_LIMIT),
        name="in_proj",
    )(layer, x, w_in)


def _proj_kv_kernel(l_ref, x_ref, w_ref, k_in_ref, v_in_ref, k_ref, v_ref, kv_ref, xb_ref):
    n = pl.program_id(1)

    @pl.when(n == 0)
    def _():
        xb_ref[...] = x_ref[...].astype(BF16)

    def store_heads(dst_ref):
        kv_ref[...] = _dot(xb_ref[...], w_ref[...].astype(BF16))
        for h in range(N_HEADS_A):
            dst_ref[:, h, :] = kv_ref[:, h * HEAD_DIM:(h + 1) * HEAD_DIM]

    @pl.when(n == 0)
    def _():
        store_heads(k_ref)

    @pl.when(n == 1)
    def _():
        store_heads(v_ref)


def _in_proj_kv(layer, x, w_in, k_all, v_all, tm):
    m = x.shape[0]
    out_spec = pl.BlockSpec((None, tm, N_HEADS_A, HEAD_DIM), lambda i, n, l: (l[0], i, 0, 0))
    dense_spec = pl.BlockSpec((tm, WIDTH_A), lambda i, n, l: (i, n))
    return pl.pallas_call(
        _proj_kv_kernel,
        grid_spec=pltpu.PrefetchScalarGridSpec(
            num_scalar_prefetch=1,
            grid=(m // tm, 2),
            in_specs=[
                pl.BlockSpec((tm, D_MODEL), lambda i, n, l: (i, 0)),
                pl.BlockSpec((None, D_MODEL, WIDTH_A), lambda i, n, l: (l[0], 0, 1 + n)),
                pl.BlockSpec(memory_space=pl.ANY),
                pl.BlockSpec(memory_space=pl.ANY),
            ],
            out_specs=[out_spec, out_spec, dense_spec],
            scratch_shapes=[pltpu.VMEM((tm, D_MODEL), BF16)],
        ),
        out_shape=[jax.ShapeDtypeStruct(k_all.shape, F32), jax.ShapeDtypeStruct(v_all.shape, F32),
                   jax.ShapeDtypeStruct((m, 2 * WIDTH_A), F32)],
        input_output_aliases={3: 0, 4: 1},
        compiler_params=pltpu.CompilerParams(
            dimension_semantics=("parallel", "arbitrary"), vmem_limit_bytes=V7X_VMEM_LIMIT),
        name="in_proj_kv",
    )(layer, x, w_in, k_all, v_all)


def _out_proj_kernel(l_ref, oa_ref, ob_ref, x_ref, w_ref, g_ref, b_ref, o_ref, wb_ref):
    @pl.when(pl.program_id(0) == 0)
    def _():
        wb_ref[...] = w_ref[...].astype(BF16)

    o_ref[...] = _dot(oa_ref[...], wb_ref[0:WIDTH_A, :]) + _dot(ob_ref[...], wb_ref[WIDTH_A:, :])
    _residual_ln_epilogue(x_ref, o_ref, g_ref, b_ref, 1.0)


def _out_proj_ln(layer, oa, ob, x, w_out, ln_g, ln_b, tm):
    m = x.shape[0]
    return pl.pallas_call(
        _out_proj_kernel,
        grid_spec=pltpu.PrefetchScalarGridSpec(
            num_scalar_prefetch=1,
            grid=(m // tm,),
            in_specs=[
                pl.BlockSpec((tm, WIDTH_A), lambda i, l: (i, 0)),
                pl.BlockSpec((tm, WIDTH_BV), lambda i, l: (i, 0)),
                pl.BlockSpec((tm, D_MODEL), lambda i, l: (i, 0)),
                pl.BlockSpec((None, WIDTH_A + WIDTH_BV, D_MODEL), lambda i, l: (l[0], 0, 0),
                             pipeline_mode=pl.Buffered(1)),
                pl.BlockSpec((None, None, 1, D_MODEL), lambda i, l: (l[0], 1, 0, 0)),
                pl.BlockSpec((None, None, 1, D_MODEL), lambda i, l: (l[0], 1, 0, 0)),
            ],
            out_specs=pl.BlockSpec((tm, D_MODEL), lambda i, l: (i, 0)),
            scratch_shapes=[pltpu.VMEM((WIDTH_A + WIDTH_BV, D_MODEL), BF16)],
        ),
        out_shape=jax.ShapeDtypeStruct((m, D_MODEL), F32),
        compiler_params=pltpu.CompilerParams(
            dimension_semantics=("arbitrary",), vmem_limit_bytes=V7X_VMEM_LIMIT),
        name="out_proj_ln",
    )(layer, oa, ob, x, w_out, ln_g, ln_b)


def _ple_kernel(l_ref, x_ref, p_ref, wp_ref, wg_ref, g_ref, b_ref, o_ref, wpb_ref, wgb_ref):
    @pl.when(pl.program_id(0) == 0)
    def _():
        wpb_ref[...] = wp_ref[...].astype(BF16)
        wgb_ref[...] = wg_ref[...].astype(BF16)

    gate = jax.nn.sigmoid(_dot(x_ref[...].astype(BF16), wgb_ref[...]))
    o_ref[...] = _dot(p_ref[...].astype(BF16), wpb_ref[...]) * gate
    _residual_ln_epilogue(x_ref, o_ref, g_ref, b_ref, 1.0)


def _ple_ln(layer, x, p, w_ple, w_gate, ln_g, ln_b, tm):
    m = x.shape[0]
    return pl.pallas_call(
        _ple_kernel,
        grid_spec=pltpu.PrefetchScalarGridSpec(
            num_scalar_prefetch=1,
            grid=(m // tm,),
            in_specs=[
                pl.BlockSpec((tm, D_MODEL), lambda i, l: (i, 0)),
                pl.BlockSpec((None, tm, PLE_DIM), lambda i, l: (l[0], i, 0)),
                pl.BlockSpec((None, PLE_DIM, D_MODEL), lambda i, l: (l[0], 0, 0),
                             pipeline_mode=pl.Buffered(1)),
                pl.BlockSpec((None, D_MODEL, D_MODEL), lambda i, l: (l[0], 0, 0),
                             pipeline_mode=pl.Buffered(1)),
                pl.BlockSpec((None, None, 1, D_MODEL), lambda i, l: (l[0], 3, 0, 0)),
                pl.BlockSpec((None, None, 1, D_MODEL), lambda i, l: (l[0], 3, 0, 0)),
            ],
            out_specs=pl.BlockSpec((tm, D_MODEL), lambda i, l: (i, 0)),
            scratch_shapes=[pltpu.VMEM((PLE_DIM, D_MODEL), BF16),
                            pltpu.VMEM((D_MODEL, D_MODEL), BF16)],
        ),
        out_shape=jax.ShapeDtypeStruct((m, D_MODEL), F32),
        compiler_params=pltpu.CompilerParams(
            dimension_semantics=("arbitrary",), vmem_limit_bytes=V7X_VMEM_LIMIT),
        name="ple_ln",
    )(layer, x, p, w_ple, w_gate, ln_g, ln_b)


def _merge_pair(o_a, l_a, o_b, l_b):
    m = jnp.maximum(l_a, l_b)
    e_a = jnp.exp(l_a - m)
    e_b = jnp.exp(l_b - m)
    den = e_a + e_b
    return (e_a * o_a + e_b * o_b) / den, m + jnp.log(den)


def _attn_prompt_kernel(l_ref, slope_ref, q_ref, k_ref, v_ref, o_ref,
                        x4_ref, x16_ref, va_ref, s_ref, p_ref, ls_ref, bias_ref, og_ref, lse_ref,
                        *, seq):
    nblk = seq // BAND
    slab = ATTN_GROUP * BAND
    part4 = seq // 4
    slope = slope_ref[pl.program_id(1)]

    nat = (q_ref, k_ref, v_ref)
    for t in range(3):
        for a in range(4):
            for c in range(part4 // BAND):
                x4_ref[t, pl.ds(a * part4 + c * BAND, BAND), :] = (
                    nat[t][pl.ds(a + 4 * c * BAND, BAND, stride=4), :])
    for t in range(3):
        for a in range(4):
            for b in range(4):
                x16_ref[t, pl.ds((a + 4 * b) * BAND, BAND), :] = (
                    x4_ref[t, pl.ds(a * part4 + b, BAND, stride=4), :])

    ones = jnp.ones((slab, HEAD_DIM), BF16)
    for slot, src in enumerate((lambda r: v_ref[r, :], lambda r: x4_ref[2, r, :],
                                lambda r: x16_ref[2, r, :])):
        for c in range(seq // slab):
            r = pl.ds(c * slab, slab)
            va_ref[slot, r, 0:HEAD_DIM] = src(r).astype(BF16)
            va_ref[slot, r, HEAD_DIM:2 * HEAD_DIM] = ones

    row = lax.broadcasted_iota(jnp.int32, (slab, 2 * BAND), 0)
    col = lax.broadcasted_iota(jnp.int32, (slab, 2 * BAND), 1)
    dist = jnp.bitwise_and(row, BAND - 1) + BAND - col
    in_band = (dist >= 0) & (dist <= BAND)
    starts_class = (row < BAND) & (col < BAND)
    distf = dist.astype(F32)
    tile1 = jnp.where(in_band, -(slope * (1 * LOG2E)) * distf, NEG_INF)
    bias_ref[1] = tile1
    bias_ref[0] = jnp.where(starts_class, NEG_INF, tile1)
    bias_ref[2] = jnp.where(starts_class | ~in_band, NEG_INF, -(slope * (4 * LOG2E)) * distf)
    dist_own = dist[:, BAND:]
    bias_ref[3, :, 0:BAND] = jnp.where(dist_own >= 0,
                                       -(slope * (16 * LOG2E)) * dist_own.astype(F32), NEG_INF)

    def band_pass(qf, kf, vf, two, first_tile, rest_tile, store):
        kw = 2 * BAND if two else BAND

        if two:
            s_ref[0:BAND, 0:BAND] = jnp.zeros((BAND, BAND), F32)
            s_ref[0:BAND, BAND:kw] = _dot_nt(qf(0, BAND).astype(BF16), kf(0, BAND).astype(BF16))

            def scores(j, carry):
                r = pl.multiple_of(j * BAND, BAND)
                rp = pl.multiple_of(r - BAND, BAND)
                s_ref[pl.ds(r, BAND), :] = _dot_nt(qf(r, BAND).astype(BF16),
                                                   kf(rp, kw).astype(BF16))
                return carry

            lax.fori_loop(1, nblk, scores, 0, unroll=ATTN_UNROLL)
        else:
            def scores(j, carry):
                r = pl.multiple_of(j * BAND, BAND)
                s_ref[pl.ds(r, BAND), 0:kw] = _dot_nt(qf(r, BAND).astype(BF16),
                                                      kf(r, BAND).astype(BF16))
                return carry

            lax.fori_loop(0, nblk, scores, 0, unroll=ATTN_UNROLL)

        def softmax_slab(r, tile):
            s = s_ref[pl.ds(r, slab), 0:kw] * (ATTN_SCALE * LOG2E) + bias_ref[tile, :, 0:kw]
            m = jnp.max(s, axis=-1, keepdims=True)
            p = jnp.exp2(s - m)
            p_ref[pl.ds(r, slab), 0:kw] = p.astype(BF16)
            ls_ref[pl.ds(r, slab), :] = jnp.broadcast_to(m * LN2, (slab, HEAD_DIM))

        first_rest = 0
        if first_tile != rest_tile:
            softmax_slab(0, first_tile)
            first_rest = 1

        def rest_slab(g, carry):
            softmax_slab(pl.multiple_of(g * slab, slab), rest_tile)
            return carry

        lax.fori_loop(first_rest, nblk // ATTN_GROUP, rest_slab, 0)

        def finish(j, r, o_l):
            l = o_l[:, HEAD_DIM:]
            store(j, o_l[:, 0:HEAD_DIM] / l, ls_ref[pl.ds(r, BAND), :] + jnp.log(l))

        if two:
            finish(0, 0, _dot(p_ref[0:BAND, BAND:kw], vf(0, BAND)))

            def values(j, carry):
                r = pl.multiple_of(j * BAND, BAND)
                rp = pl.multiple_of(r - BAND, BAND)
                finish(j, r, _dot(p_ref[pl.ds(r, BAND), :], vf(rp, kw)))
                return carry

            lax.fori_loop(1, nblk, values, 0, unroll=ATTN_UNROLL)
        else:
            def values(j, carry):
                r = pl.multiple_of(j * BAND, BAND)
                finish(j, r, _dot(p_ref[pl.ds(r, BAND), 0:kw], vf(r, BAND)))
                return carry

            lax.fori_loop(0, nblk, values, 0, unroll=ATTN_UNROLL)

    def store_contiguous(slot):
        def store(j, o, lse):
            og_ref[slot, pl.ds(j * BAND, BAND), :] = o
            lse_ref[slot, pl.ds(j * BAND, BAND), :] = lse
        return store

    def store_16_as_4(j, o, lse):
        start = jnp.bitwise_and(j, 3) * part4 + jnp.right_shift(j, 2)
        og_ref[2, pl.ds(start, BAND, stride=4), :] = o
        lse_ref[2, pl.ds(start, BAND, stride=4), :] = lse

    values_of = lambda slot: (lambda start, size: va_ref[slot, pl.ds(start, size), :])
    natural = [lambda start, size, ref=ref: ref[pl.ds(start, size), :] for ref in nat[:2]]
    by4 = [lambda start, size, t=t: x4_ref[t, pl.ds(start, size), :] for t in range(2)]
    by16 = [lambda start, size, t=t: x16_ref[t, pl.ds(start, size), :] for t in range(2)]
    natural.append(values_of(0))
    by4.append(values_of(1))
    by16.append(values_of(2))
    band_pass(*natural, True, 0, 1, store_contiguous(0))
    band_pass(*by4, True, 2, 2, store_contiguous(1))
    band_pass(*by16, False, 3, 3, store_16_as_4)

    for a in range(4):
        for c in range(part4 // BAND):
            src = pl.ds(a * part4 + c * BAND, BAND)
            dst = pl.ds(a + 4 * c * BAND, BAND, stride=4)
            o, lse = _merge_pair(og_ref[1, src, :], lse_ref[1, src, :],
                                 og_ref[2, src, :], lse_ref[2, src, :])
            og_ref[3, dst, :] = o
            lse_ref[3, dst, :] = lse

    def final_merge(c, carry):
        rows = pl.ds(pl.multiple_of(c * BAND, BAND), BAND)
        o, _ = _merge_pair(og_ref[0, rows, :], lse_ref[0, rows, :],
                           og_ref[3, rows, :], lse_ref[3, rows, :])
        o_ref[rows, :] = o.astype(BF16)
        return carry

    lax.fori_loop(0, nblk, final_merge, 0, unroll=2)


def _attn_prompt(layer, zqb, kv, slopes, batch, seq):
    slab = ATTN_GROUP * BAND
    return pl.pallas_call(
        functools.partial(_attn_prompt_kernel, seq=seq),
        grid_spec=pltpu.PrefetchScalarGridSpec(
            num_scalar_prefetch=1,
            grid=(batch, N_HEADS_A),
            in_specs=[
                pl.BlockSpec(memory_space=pltpu.SMEM),
                pl.BlockSpec((seq, HEAD_DIM), lambda b, h, l: (b, h)),
                pl.BlockSpec((seq, HEAD_DIM), lambda b, h, l: (b, h)),
                pl.BlockSpec((seq, HEAD_DIM), lambda b, h, l: (b, N_HEADS_A + h)),
            ],
            out_specs=pl.BlockSpec((seq, HEAD_DIM), lambda b, h, l: (b, h)),
            scratch_shapes=[
                pltpu.VMEM((3, seq, HEAD_DIM), F32),
                pltpu.VMEM((3, seq, HEAD_DIM), F32),
                pltpu.VMEM((3, seq, 2 * HEAD_DIM), BF16),
                pltpu.VMEM((seq, 2 * BAND), F32),
                pltpu.VMEM((seq, 2 * BAND), BF16),
                pltpu.VMEM((seq, HEAD_DIM), F32),
                pltpu.VMEM((4, slab, 2 * BAND), F32),
                pltpu.VMEM((4, seq, HEAD_DIM), F32),
                pltpu.VMEM((4, seq, HEAD_DIM), F32),
            ],
        ),
        out_shape=jax.ShapeDtypeStruct((batch * seq, WIDTH_A), BF16),
        compiler_params=pltpu.CompilerParams(
            dimension_semantics=("parallel", "parallel"), vmem_limit_bytes=V7X_VMEM_LIMIT),
        name="attn_prompt",
    )(layer, slopes, zqb, kv, kv)


def _group_norm_gate(o, gate, gn_g, gn_b):
    mu = jnp.mean(o, axis=-1, keepdims=True)
    d = o - mu
    var = jnp.mean(d * d, axis=-1, keepdims=True)
    on = d * lax.rsqrt(var + GN_EPS)
    return (on * gn_g + gn_b) * (gate * jax.nn.sigmoid(gate))


def _ret_prompt_kernel(l_ref, lg_ref, q_ref, k_ref, v_ref, gate_ref, gng_ref, gnb_ref, o_ref,
                       s_ref, st_ref, *, seq):
    pair = pl.program_id(1)
    c_len = RET_CHUNK
    lane = lax.broadcasted_iota(jnp.int32, (c_len, 2 * DK_B), 1)
    pos_r = lax.broadcasted_iota(jnp.int32, (c_len, c_len), 0)
    pos_c = lax.broadcasted_iota(jnp.int32, (c_len, c_len), 1)
    rel = (pos_r - pos_c).astype(F32)
    pos = lax.broadcasted_iota(jnp.int32, (c_len, 1), 0).astype(F32)
    ones11 = jnp.ones((1, 1), F32)

    heads = []
    for hx in range(2):
        lg = lg_ref[2 * pair + hx]
        decay_mask = jnp.where(rel >= 0, jnp.exp(jnp.maximum(rel, 0.0) * lg), 0.0)
        q_decay = jnp.exp((pos + 1.0) * lg)
        k_decay = jnp.exp((c_len - 1.0 - pos) * lg)
        chunk_decay = jnp.exp((c_len * ones11) * lg)
        heads.append((decay_mask, q_decay, k_decay, chunk_decay))

    st_ref[...] = jnp.zeros_like(st_ref)

    def chunk(c, carry):
        for bi in range(RET_BATCH):
            r = pl.multiple_of(bi * seq + c * c_len, c_len)
            q2 = q_ref[pl.ds(r, c_len), :]
            k2 = k_ref[pl.ds(r, c_len), :] * KB_SCALE
            kb = k2.astype(BF16)
            for hx in range(2):
                decay_mask, q_decay, k_decay, chunk_decay = heads[hx]
                cols = slice(hx * DV_B, (hx + 1) * DV_B)
                qx = jnp.where((lane >= DK_B) if hx else (lane < DK_B), q2, 0.0).astype(BF16)
                vx = v_ref[pl.ds(r, c_len), cols].astype(BF16)
                state = st_ref[2 * bi + hx]
                inner = _dot_nt(qx, kb) * decay_mask
                o = _dot(inner.astype(BF16), vx) + _dot(qx, state.astype(BF16)) * q_decay
                kd_t = (k2 * k_decay).T.astype(BF16)
                st_ref[2 * bi + hx] = state * chunk_decay + _dot(kd_t, vx)
                res = _group_norm_gate(o, gate_ref[pl.ds(r, c_len), cols],
                                       gng_ref[:, cols], gnb_ref[:, cols])
                o_ref[pl.ds(r, c_len), cols] = res.astype(BF16)
        return carry

    lax.fori_loop(0, seq // c_len, chunk, 0)
    for bi in range(RET_BATCH):
        s_ref[bi, 0] = st_ref[2 * bi, 0:DK_B, :]
        s_ref[bi, 1] = st_ref[2 * bi + 1, DK_B:2 * DK_B, :]


def _ret_prompt(layer, z, log_g, gn_g, gn_b, batch, seq):
    assert batch % RET_BATCH == 0
    rows = RET_BATCH * seq
    qcol = WIDTH_A // (2 * DK_B)
    kcol = qcol + WIDTH_BQK // (2 * DK_B)
    vcol = (WIDTH_A + 2 * WIDTH_BQK) // (2 * DV_B)
    gcol = vcol + WIDTH_BV // (2 * DV_B)
    return pl.pallas_call(
        functools.partial(_ret_prompt_kernel, seq=seq),
        grid_spec=pltpu.PrefetchScalarGridSpec(
            num_scalar_prefetch=1,
            grid=(batch // RET_BATCH, N_HEADS_B // 2),
            in_specs=[
                pl.BlockSpec(memory_space=pltpu.SMEM),
                pl.BlockSpec((rows, 2 * DK_B), lambda b, p, l: (b, qcol + p)),
                pl.BlockSpec((rows, 2 * DK_B), lambda b, p, l: (b, kcol + p)),
                pl.BlockSpec((rows, 2 * DV_B), lambda b, p, l: (b, vcol + p)),
                pl.BlockSpec((rows, 2 * DV_B), lambda b, p, l: (b, gcol + p)),
                pl.BlockSpec((None, 1, 2 * DV_B), lambda b, p, l: (l[0], 0, p)),
                pl.BlockSpec((None, 1, 2 * DV_B), lambda b, p, l: (l[0], 0, p)),
            ],
            out_specs=[
                pl.BlockSpec((rows, 2 * DV_B), lambda b, p, l: (b, p)),
                pl.BlockSpec((RET_BATCH, 2, DK_B, DV_B), lambda b, p, l: (b, p, 0, 0)),
            ],
            scratch_shapes=[pltpu.VMEM((2 * RET_BATCH, 2 * DK_B, DV_B), F32)],
        ),
        out_shape=[jax.ShapeDtypeStruct((batch * seq, WIDTH_BV), BF16),
                   jax.ShapeDtypeStruct((batch, N_HEADS_B, DK_B, DV_B), F32)],
        compiler_params=pltpu.CompilerParams(
            dimension_semantics=("parallel", "parallel"), vmem_limit_bytes=V7X_VMEM_LIMIT),
        name="ret_prompt",
    )(layer, log_g, z, z, z, z, gn_g, gn_b)


def _attn_decode_kernel(l_ref, slope_ref, z_ref, k1_ref, k4_ref, k16_ref, v1_ref, v4_ref, v16_ref,
                        o_ref):
    row = z_ref[0]
    steps = (BAND - lax.broadcasted_iota(jnp.int32, (BAND, 1), 0)).astype(F32)
    views = ((k1_ref, v1_ref, 1), (k4_ref, v4_ref, 4), (k16_ref, v16_ref, 16))
    for h in range(N_HEADS_A):
        cols = slice(h * HEAD_DIM, (h + 1) * HEAD_DIM)
        qh = row[:, cols]
        k_new = row[:, WIDTH_A + h * HEAD_DIM:WIDTH_A + (h + 1) * HEAD_DIM]
        v_new = row[:, 2 * WIDTH_A + h * HEAD_DIM:2 * WIDTH_A + (h + 1) * HEAD_DIM]
        slope = slope_ref[h]
        s_new = jnp.sum(qh * k_new, axis=1, keepdims=True) * ATTN_SCALE
        outs, lses = [], []
        for k_ref, v_ref, dil in views:
            s = (jnp.sum(k_ref[:, h, :] * qh, axis=1, keepdims=True) * ATTN_SCALE
                 - slope * (steps * dil))
            m = jnp.maximum(jnp.max(s, axis=0, keepdims=True), s_new)
            e = jnp.exp(s - m)
            e_new = jnp.exp(s_new - m)
            l = jnp.sum(e, axis=0, keepdims=True) + e_new
            o = (jnp.sum(e * v_ref[:, h, :], axis=0, keepdims=True) + e_new * v_new) / l
            outs.append(o)
            lses.append(m + jnp.log(l))
        m = jnp.maximum(jnp.maximum(lses[0], lses[1]), lses[2])
        w = [jnp.exp(x - m) for x in lses]
        o = (w[0] * outs[0] + w[1] * outs[1] + w[2] * outs[2]) / (w[0] + w[1] + w[2])
        o_ref[0, :, cols] = o.astype(BF16)


def _attn_decode(layer, slopes, z3, cache_k, cache_v):
    depth, nb, n_past = cache_k.shape[0], cache_k.shape[1], cache_k.shape[2]
    specs, args = [], []
    for cache in (cache_k, cache_v):
        for _, dil in DILATIONS:
            view = cache.reshape(depth, nb, n_past // dil, dil, N_HEADS_A, HEAD_DIM)
            last = n_past // dil // BAND - 1
            specs.append(pl.BlockSpec((None, None, BAND, None, N_HEADS_A, HEAD_DIM),
                                      lambda b, l, last=last: (l[0], b, last, 0, 0, 0)))
            args.append(view)
    return pl.pallas_call(
        _attn_decode_kernel,
        grid_spec=pltpu.PrefetchScalarGridSpec(
            num_scalar_prefetch=1,
            grid=(nb,),
            in_specs=[
                pl.BlockSpec(memory_space=pltpu.SMEM),
                pl.BlockSpec((1, 1, 3 * WIDTH_A), lambda b, l: (b, 0, 0)),
            ] + specs,
            out_specs=pl.BlockSpec((1, 1, WIDTH_A), lambda b, l: (b, 0, 0)),
        ),
        out_shape=jax.ShapeDtypeStruct((nb, 1, WIDTH_A), BF16),
        compiler_params=pltpu.CompilerParams(
            dimension_semantics=("parallel",), vmem_limit_bytes=V7X_VMEM_LIMIT),
        name="attn_decode",
    )(layer, slopes, z3, *args)


def _ret_decode_kernel(l_ref, lg_ref, z_ref, st_ref, gng_ref, gnb_ref, o_ref, sn_ref):
    row = z_ref[0]
    eye = (lax.broadcasted_iota(jnp.int32, (DK_B, DK_B), 0)
           == lax.broadcasted_iota(jnp.int32, (DK_B, DK_B), 1))
    ones11 = jnp.ones((1, 1), F32)
    v0 = 2 * WIDTH_BQK
    g0 = v0 + WIDTH_BV
    for h in range(N_HEADS_B):
        qh = row[:, h * DK_B:(h + 1) * DK_B]
        kh = row[:, WIDTH_BQK + h * DK_B:WIDTH_BQK + (h + 1) * DK_B] * KB_SCALE
        vh = row[:, v0 + h * DV_B:v0 + (h + 1) * DV_B]
        gh = row[:, g0 + h * DV_B:g0 + (h + 1) * DV_B]
        cols = slice(h * DV_B, (h + 1) * DV_B)
        gamma = jnp.exp(ones11 * lg_ref[h])
        q_col = jnp.sum(jnp.where(eye, qh, 0.0), axis=1, keepdims=True)
        k_col = jnp.sum(jnp.where(eye, kh, 0.0), axis=1, keepdims=True)
        state = st_ref[h]
        qk = jnp.sum(qh * kh, axis=1, keepdims=True)
        o = qk * vh + jnp.sum(q_col * state, axis=0, keepdims=True) * gamma
        sn_ref[h] = state * gamma + k_col * vh
        res = _group_norm_gate(o, gh, gng_ref[:, cols], gnb_ref[:, cols])
        o_ref[0, :, cols] = res.astype(BF16)


def _ret_decode(layer, log_g, z3, state_ret, gn_g, gn_b):
    nb = state_ret.shape[1]
    width = 2 * WIDTH_BQK + 2 * WIDTH_BV
    return pl.pallas_call(
        _ret_decode_kernel,
        grid_spec=pltpu.PrefetchScalarGridSpec(
            num_scalar_prefetch=1,
            grid=(nb,),
            in_specs=[
                pl.BlockSpec(memory_space=pltpu.SMEM),
                pl.BlockSpec((1, 1, width), lambda b, l: (b, 0, 1)),
                pl.BlockSpec((None, None, N_HEADS_B, DK_B, DV_B), lambda b, l: (l[0], b, 0, 0, 0)),
                pl.BlockSpec((None, 1, WIDTH_BV), lambda b, l: (l[0], 0, 0)),
                pl.BlockSpec((None, 1, WIDTH_BV), lambda b, l: (l[0], 0, 0)),
            ],
            out_specs=[
                pl.BlockSpec((1, 1, WIDTH_BV), lambda b, l: (b, 0, 0)),
                pl.BlockSpec((None, N_HEADS_B, DK_B, DV_B), lambda b, l: (b, 0, 0, 0)),
            ],
        ),
        out_shape=[jax.ShapeDtypeStruct((nb, 1, WIDTH_BV), BF16),
                   jax.ShapeDtypeStruct((nb, N_HEADS_B, DK_B, DV_B), F32)],
        compiler_params=pltpu.CompilerParams(
            dimension_semantics=("parallel",), vmem_limit_bytes=V7X_VMEM_LIMIT),
        name="ret_decode",
    )(layer, log_g, z3, state_ret, gn_g, gn_b)


def kernel(x_prompt, x_sample, cache_k, cache_v, state_ret, p_prompt, p_sample, w_in, w_out, gn_g,
           gn_b, ffn1_w1, ffn1_w3, ffn1_w2, ffn2_w1, ffn2_w3, ffn2_w2, w_ple, w_gate, ln_g, ln_b):
    batch, seq, _ = x_prompt.shape
    dec_batch, dec_seq, _ = x_sample.shape
    assert dec_seq == 1 and seq == BAND * DILATIONS[-1][1]
    assert cache_k.shape[2] == BAND * DILATIONS[-1][1]
    m_p, m_s = batch * seq, dec_batch * dec_seq
    tm_p = 1024
    assert m_p % tm_p == 0

    f1 = (ffn1_w1, ffn1_w3, ffn1_w2)
    f2 = (ffn2_w1, ffn2_w3, ffn2_w2)
    ln_g4 = ln_g.reshape(DEPTH, 4, 1, D_MODEL)
    ln_b4 = ln_b.reshape(DEPTH, 4, 1, D_MODEL)
    gn_g3 = gn_g.reshape(DEPTH, 1, WIDTH_BV)
    gn_b3 = gn_b.reshape(DEPTH, 1, WIDTH_BV)
    p_p = p_prompt.reshape(DEPTH, m_p, PLE_DIM)
    p_s = p_sample.reshape(DEPTH, m_s, PLE_DIM)

    heads_a = jnp.arange(1, N_HEADS_A + 1, dtype=F32)
    slopes = jnp.exp2(-8.0 * heads_a / N_HEADS_A)
    log_g = jnp.log(1.0 - jnp.exp2(-5.0 - jnp.arange(N_HEADS_B, dtype=F32)))

    def layer_step(carry, i):
        xp, xs, k_p, v_p = carry
        layer = i.reshape(1)

        xp, xs = _ffn_ln(layer, xp, xs, *f1, ln_g4, ln_b4, 0, tm_p)

        zp = _in_proj(layer, xp, w_in, tm_p, WIDTH_A, 4, skip_after_first=2)
        k_p, v_p, kv_p = _in_proj_kv(layer, xp, w_in, k_p, v_p, 512)
        zs = _in_proj(layer, xs, w_in, m_s, WIDTH_A, IN_WIDTH // WIDTH_A)

        oa_p = _attn_prompt(layer, zp, kv_p, slopes, batch, seq)
        ob_p, ret_p = _ret_prompt(layer, zp, log_g, gn_g3, gn_b3, batch, seq)
        zs3 = zs.reshape(m_s, 1, IN_WIDTH)
        oa_s = _attn_decode(layer, slopes, zs3, cache_k, cache_v).reshape(m_s, WIDTH_A)
        ob_s, ret_s = _ret_decode(layer, log_g, zs3, state_ret, gn_g3, gn_b3)
        ob_s = ob_s.reshape(m_s, WIDTH_BV)

        xp = _out_proj_ln(layer, oa_p, ob_p, xp, w_out, ln_g4, ln_b4, 512)
        xs = _out_proj_ln(layer, oa_s, ob_s, xs, w_out, ln_g4, ln_b4, m_s)

        xp, xs = _ffn_ln(layer, xp, xs, *f2, ln_g4, ln_b4, 2, tm_p)

        xp = _ple_ln(layer, xp, p_p, w_ple, w_gate, ln_g4, ln_b4, 512)
        xs = _ple_ln(layer, xs, p_s, w_ple, w_gate, ln_g4, ln_b4, m_s)

        k_s = zs[:, WIDTH_A:2 * WIDTH_A].reshape(dec_batch, dec_seq, N_HEADS_A, HEAD_DIM)
        v_s = zs[:, 2 * WIDTH_A:3 * WIDTH_A].reshape(dec_batch, dec_seq, N_HEADS_A, HEAD_DIM)
        return (xp, xs, k_p, v_p), (ret_p, k_s, v_s, ret_s)

    kv_shape = (DEPTH, m_p, N_HEADS_A, HEAD_DIM)
    (xp, xs, k_p, v_p), (ret_p, k_s, v_s, ret_s) = lax.scan(
        layer_step,
        (x_prompt.reshape(m_p, D_MODEL), x_sample.reshape(m_s, D_MODEL),
         jnp.zeros(kv_shape, F32), jnp.zeros(kv_shape, F32)),
        jnp.arange(DEPTH, dtype=jnp.int32))
    kv_out = (DEPTH, batch, seq, N_HEADS_A, HEAD_DIM)
    return (xp.reshape(batch, seq, D_MODEL), xs.reshape(dec_batch, dec_seq, D_MODEL),
            k_p.reshape(kv_out), v_p.reshape(kv_out), ret_p, k_s, v_s, ret_s)
```

```python
import functools

import jax
import jax.numpy as jnp
from jax import lax
from jax.experimental import pallas as pl
from jax.experimental.pallas import tpu as pltpu

F32 = jnp.float32
BF16 = jnp.bfloat16

D_MODEL = 2048
DEPTH = 4
HEAD_DIM = 128
WIDTH_A = 1024
N_HEADS_A = 8
DILATIONS = ((128, 1), (512, 4), (2048, 16))
N_HEADS_B = 8
DV_B = 128
DK_B = 64
WIDTH_BQK = N_HEADS_B * DK_B
WIDTH_BV = N_HEADS_B * DV_B
IN_WIDTH = 3 * WIDTH_A + 2 * WIDTH_BQK + 2 * WIDTH_BV
FFN_DIM = 5632
PLE_DIM = 256
RET_CHUNK = 128
LN_EPS = 1e-5
GN_EPS = 1e-5
NEG_INF = -1e30
DEEPNORM_ALPHA = (2 * DEPTH) ** 0.25
ATTN_SCALE = HEAD_DIM ** -0.5
KB_SCALE = DK_B ** -0.5

BAND = 128
V7X_VMEM_LIMIT = 56 * 1024 * 1024
V7X_VMEM_LIMIT_FFN = 62 * 1024 * 1024
FFN_TILE_F = 512
LN_ROWS = 128
ATTN_UNROLL = 4
ATTN_GROUP = 4
RET_BATCH = 2
LOG2E = 1.4426950408889634
LN2 = 0.6931471805599453


def _dot(a, b):
    return jnp.dot(a, b, preferred_element_type=F32)


def _dot_nt(a, b):
    return lax.dot_general(a, b, (((1,), (1,)), ((), ())), preferred_element_type=F32)


def _layer_norm_rows(y, g, b):
    mu = jnp.mean(y, axis=-1, keepdims=True)
    d = y - mu
    var = jnp.mean(d * d, axis=-1, keepdims=True)
    return d * lax.rsqrt(var + LN_EPS) * g + b


def _residual_ln_epilogue(x_ref, o_ref, g_ref, b_ref, branch_scale):
    rows = o_ref.shape[0]
    step = min(rows, LN_ROWS)
    g = g_ref[...]
    b = b_ref[...]

    def body(c, carry):
        r = pl.multiple_of(c * step, step)
        y = DEEPNORM_ALPHA * x_ref[pl.ds(r, step), :] + branch_scale * o_ref[pl.ds(r, step), :]
        o_ref[pl.ds(r, step), :] = _layer_norm_rows(y, g, b)
        return carry

    lax.fori_loop(0, rows // step, body, 0)


def _swiglu_step(xb, w1b, w3b, w2b):
    h1 = _dot(xb, w1b)
    h3 = _dot(xb, w3b)
    act = (h1 * jax.nn.sigmoid(h1) * h3).astype(BF16)
    return _dot(act, w2b)


def _ffn_kernel(l_ref, x_ref, xs_ref, w1_ref, w3_ref, w2_ref, c1_ref, c3_ref, c2_ref, g_ref, b_ref,
                o_ref, os_ref, n1_ref, n3_ref, n2_ref, xb_ref, xsb_ref, *, nf):
    i = pl.program_id(0)
    f = pl.program_id(1)

    @pl.when(f == 0)
    def _():
        xb_ref[...] = x_ref[...].astype(BF16)
        o_ref[...] = jnp.zeros_like(o_ref)

    @pl.when((f == 0) & (i == 0))
    def _():
        xsb_ref[...] = xs_ref[...].astype(BF16)
        os_ref[...] = jnp.zeros_like(os_ref)

    n1_ref[...] = c1_ref[...].astype(BF16)
    n3_ref[...] = c3_ref[...].astype(BF16)
    n2_ref[...] = c2_ref[...].astype(BF16)

    o_ref[...] += _swiglu_step(xb_ref[...], w1_ref[...], w3_ref[...], w2_ref[...])

    @pl.when(i == 0)
    def _():
        os_ref[...] += _swiglu_step(xsb_ref[...], w1_ref[...], w3_ref[...], w2_ref[...])

    @pl.when(f == nf - 1)
    def _():
        _residual_ln_epilogue(x_ref, o_ref, g_ref, b_ref, 0.5)

    @pl.when((f == nf - 1) & (i == 0))
    def _():
        _residual_ln_epilogue(xs_ref, os_ref, g_ref, b_ref, 0.5)


def _ffn_ln(layers, x, xs, w1, w3, w2, c1, c3, c2, ln_g, ln_b, ln_idx, tm):
    m, ms = x.shape[0], xs.shape[0]
    nf = FFN_DIM // FFN_TILE_F
    cr = D_MODEL // (m // tm)
    return pl.pallas_call(
        functools.partial(_ffn_kernel, nf=nf),
        grid_spec=pltpu.PrefetchScalarGridSpec(
            num_scalar_prefetch=1,
            grid=(m // tm, nf),
            in_specs=[
                pl.BlockSpec((tm, D_MODEL), lambda i, f, l: (i, 0)),
                pl.BlockSpec((ms, D_MODEL), lambda i, f, l: (0, 0)),
                pl.BlockSpec((D_MODEL, FFN_TILE_F), lambda i, f, l: (0, f)),
                pl.BlockSpec((D_MODEL, FFN_TILE_F), lambda i, f, l: (0, f)),
                pl.BlockSpec((FFN_TILE_F, D_MODEL), lambda i, f, l: (f, 0)),
                pl.BlockSpec((None, cr, FFN_TILE_F), lambda i, f, l: (l[1], i, f)),
                pl.BlockSpec((None, cr, FFN_TILE_F), lambda i, f, l: (l[1], i, f)),
                pl.BlockSpec((None, FFN_TILE_F, cr), lambda i, f, l: (l[1], f, i)),
                pl.BlockSpec((None, None, 1, D_MODEL), lambda i, f, l: (l[0], ln_idx, 0, 0)),
                pl.BlockSpec((None, None, 1, D_MODEL), lambda i, f, l: (l[0], ln_idx, 0, 0)),
            ],
            out_specs=[
                pl.BlockSpec((tm, D_MODEL), lambda i, f, l: (i, 0)),
                pl.BlockSpec((ms, D_MODEL), lambda i, f, l: (0, 0)),
                pl.BlockSpec((cr, FFN_TILE_F), lambda i, f, l: (i, f)),
                pl.BlockSpec((cr, FFN_TILE_F), lambda i, f, l: (i, f)),
                pl.BlockSpec((FFN_TILE_F, cr), lambda i, f, l: (f, i)),
            ],
            scratch_shapes=[pltpu.VMEM((tm, D_MODEL), BF16), pltpu.VMEM((ms, D_MODEL), BF16)],
        ),
        out_shape=[jax.ShapeDtypeStruct((m, D_MODEL), F32),
                   jax.ShapeDtypeStruct((ms, D_MODEL), F32),
                   jax.ShapeDtypeStruct((D_MODEL, FFN_DIM), BF16),
                   jax.ShapeDtypeStruct((D_MODEL, FFN_DIM), BF16),
                   jax.ShapeDtypeStruct((FFN_DIM, D_MODEL), BF16)],
        compiler_params=pltpu.CompilerParams(
            dimension_semantics=("arbitrary", "arbitrary"), vmem_limit_bytes=V7X_VMEM_LIMIT_FFN),
        name="ffn_ln",
    )(layers, x, xs, w1, w3, w2, c1, c3, c2, ln_g, ln_b)


def _proj_kernel(l_ref, x_ref, w_ref, o_ref, xb_ref):
    @pl.when(pl.program_id(1) == 0)
    def _():
        xb_ref[...] = x_ref[...].astype(BF16)

    o_ref[...] = _dot(xb_ref[...], w_ref[...])


def _in_proj(layer, x, w_in, tm, tn, n_blocks, skip_after_first=0):
    m = x.shape[0]
    col = lambda n: n + jnp.where(n > 0, skip_after_first, 0)
    return pl.pallas_call(
        _proj_kernel,
        grid_spec=pltpu.PrefetchScalarGridSpec(
            num_scalar_prefetch=1,
            grid=(m // tm, n_blocks),
            in_specs=[
                pl.BlockSpec((tm, D_MODEL), lambda i, n, l: (i, 0)),
                pl.BlockSpec((None, D_MODEL, tn), lambda i, n, l: (l[0], 0, col(n))),
            ],
            out_specs=pl.BlockSpec((tm, tn), lambda i, n, l: (i, n)),
            scratch_shapes=[pltpu.VMEM((tm, D_MODEL), BF16)],
        ),
        out_shape=jax.ShapeDtypeStruct((m, n_blocks * tn), F32),
        compiler_params=pltpu.CompilerParams(
            dimension_semantics=("parallel", "arbitrary"), vmem_limit_bytes=V7X_VMEM_LIMIT),
        name="in_proj",
    )(layer, x, w_in)


def _proj_kv_kernel(l_ref, x_ref, w_ref, k_in_ref, v_in_ref, k_ref, v_ref, kv_ref, xb_ref):
    n = pl.program_id(1)

    @pl.when(n == 0)
    def _():
        xb_ref[...] = x_ref[...].astype(BF16)

    def store_heads(dst_ref):
        kv_ref[...] = _dot(xb_ref[...], w_ref[...])
        for h in range(N_HEADS_A):
            dst_ref[:, h, :] = kv_ref[:, h * HEAD_DIM:(h + 1) * HEAD_DIM]

    @pl.when(n == 0)
    def _():
        store_heads(k_ref)

    @pl.when(n == 1)
    def _():
        store_heads(v_ref)


def _in_proj_kv(layer, x, w_in, k_all, v_all, tm):
    m = x.shape[0]
    out_spec = pl.BlockSpec((None, tm, N_HEADS_A, HEAD_DIM), lambda i, n, l: (l[0], i, 0, 0))
    dense_spec = pl.BlockSpec((tm, WIDTH_A), lambda i, n, l: (i, n))
    return pl.pallas_call(
        _proj_kv_kernel,
        grid_spec=pltpu.PrefetchScalarGridSpec(
            num_scalar_prefetch=1,
            grid=(m // tm, 2),
            in_specs=[
                pl.BlockSpec((tm, D_MODEL), lambda i, n, l: (i, 0)),
                pl.BlockSpec((None, D_MODEL, WIDTH_A), lambda i, n, l: (l[0], 0, 1 + n)),
                pl.BlockSpec(memory_space=pl.ANY),
                pl.BlockSpec(memory_space=pl.ANY),
            ],
            out_specs=[out_spec, out_spec, dense_spec],
            scratch_shapes=[pltpu.VMEM((tm, D_MODEL), BF16)],
        ),
        out_shape=[jax.ShapeDtypeStruct(k_all.shape, F32), jax.ShapeDtypeStruct(v_all.shape, F32),
                   jax.ShapeDtypeStruct((m, 2 * WIDTH_A), F32)],
        input_output_aliases={3: 0, 4: 1},
        compiler_params=pltpu.CompilerParams(
            dimension_semantics=("parallel", "arbitrary"), vmem_limit_bytes=V7X_VMEM_LIMIT),
        name="in_proj_kv",
    )(layer, x, w_in, k_all, v_all)


def _out_proj_kernel(l_ref, oa_ref, ob_ref, x_ref, w_ref, g_ref, b_ref, o_ref, wb_ref):
    @pl.when(pl.program_id(0) == 0)
    def _():
        wb_ref[...] = w_ref[...].astype(BF16)

    o_ref[...] = _dot(oa_ref[...], wb_ref[0:WIDTH_A, :]) + _dot(ob_ref[...], wb_ref[WIDTH_A:, :])
    _residual_ln_epilogue(x_ref, o_ref, g_ref, b_ref, 1.0)


def _out_proj_ln(layer, oa, ob, x, w_out, ln_g, ln_b, tm):
    m = x.shape[0]
    return pl.pallas_call(
        _out_proj_kernel,
        grid_spec=pltpu.PrefetchScalarGridSpec(
            num_scalar_prefetch=1,
            grid=(m // tm,),
            in_specs=[
                pl.BlockSpec((tm, WIDTH_A), lambda i, l: (i, 0)),
                pl.BlockSpec((tm, WIDTH_BV), lambda i, l: (i, 0)),
                pl.BlockSpec((tm, D_MODEL), lambda i, l: (i, 0)),
                pl.BlockSpec((None, WIDTH_A + WIDTH_BV, D_MODEL), lambda i, l: (l[0], 0, 0),
                             pipeline_mode=pl.Buffered(1)),
                pl.BlockSpec((None, None, 1, D_MODEL), lambda i, l: (l[0], 1, 0, 0)),
                pl.BlockSpec((None, None, 1, D_MODEL), lambda i, l: (l[0], 1, 0, 0)),
            ],
            out_specs=pl.BlockSpec((tm, D_MODEL), lambda i, l: (i, 0)),
            scratch_shapes=[pltpu.VMEM((WIDTH_A + WIDTH_BV, D_MODEL), BF16)],
        ),
        out_shape=jax.ShapeDtypeStruct((m, D_MODEL), F32),
        compiler_params=pltpu.CompilerParams(
            dimension_semantics=("arbitrary",), vmem_limit_bytes=V7X_VMEM_LIMIT),
        name="out_proj_ln",
    )(layer, oa, ob, x, w_out, ln_g, ln_b)


def _ple_kernel(l_ref, x_ref, p_ref, wp_ref, wg_ref, g_ref, b_ref, o_ref, wpb_ref, wgb_ref):
    @pl.when(pl.program_id(0) == 0)
    def _():
        wpb_ref[...] = wp_ref[...].astype(BF16)
        wgb_ref[...] = wg_ref[...].astype(BF16)

    gate = jax.nn.sigmoid(_dot(x_ref[...].astype(BF16), wgb_ref[...]))
    o_ref[...] = _dot(p_ref[...].astype(BF16), wpb_ref[...]) * gate
    _residual_ln_epilogue(x_ref, o_ref, g_ref, b_ref, 1.0)


def _ple_ln(layer, x, p, w_ple, w_gate, ln_g, ln_b, tm):
    m = x.shape[0]
    return pl.pallas_call(
        _ple_kernel,
        grid_spec=pltpu.PrefetchScalarGridSpec(
            num_scalar_prefetch=1,
            grid=(m // tm,),
            in_specs=[
                pl.BlockSpec((tm, D_MODEL), lambda i, l: (i, 0)),
                pl.BlockSpec((None, tm, PLE_DIM), lambda i, l: (l[0], i, 0)),
                pl.BlockSpec((None, PLE_DIM, D_MODEL), lambda i, l: (l[0], 0, 0),
                             pipeline_mode=pl.Buffered(1)),
                pl.BlockSpec((None, D_MODEL, D_MODEL), lambda i, l: (l[0], 0, 0),
                             pipeline_mode=pl.Buffered(1)),
                pl.BlockSpec((None, None, 1, D_MODEL), lambda i, l: (l[0], 3, 0, 0)),
                pl.BlockSpec((None, None, 1, D_MODEL), lambda i, l: (l[0], 3, 0, 0)),
            ],
            out_specs=pl.BlockSpec((tm, D_MODEL), lambda i, l: (i, 0)),
            scratch_shapes=[pltpu.VMEM((PLE_DIM, D_MODEL), BF16),
                            pltpu.VMEM((D_MODEL, D_MODEL), BF16)],
        ),
        out_shape=jax.ShapeDtypeStruct((m, D_MODEL), F32),
        compiler_params=pltpu.CompilerParams(
            dimension_semantics=("arbitrary",), vmem_limit_bytes=V7X_VMEM_LIMIT),
        name="ple_ln",
    )(layer, x, p, w_ple, w_gate, ln_g, ln_b)


def _merge_pair(o_a, l_a, o_b, l_b):
    m = jnp.maximum(l_a, l_b)
    e_a = jnp.exp(l_a - m)
    e_b = jnp.exp(l_b - m)
    den = e_a + e_b
    return (e_a * o_a + e_b * o_b) / den, m + jnp.log(den)


def _attn_prompt_kernel(l_ref, slope_ref, q_ref, k_ref, v_ref, o_ref,
                        x4_ref, x16_ref, va_ref, s_ref, p_ref, ls_ref, bias_ref, og_ref, lse_ref,
                        *, seq):
    nblk = seq // BAND
    slab = ATTN_GROUP * BAND
    part4 = seq // 4
    slope = slope_ref[pl.program_id(1)]

    nat = (q_ref, k_ref, v_ref)
    for t in range(3):
        for a in range(4):
            for c in range(part4 // BAND):
                x4_ref[t, pl.ds(a * part4 + c * BAND, BAND), :] = (
                    nat[t][pl.ds(a + 4 * c * BAND, BAND, stride=4), :])
    for t in range(3):
        for a in range(4):
            for b in range(4):
                x16_ref[t, pl.ds((a + 4 * b) * BAND, BAND), :] = (
                    x4_ref[t, pl.ds(a * part4 + b, BAND, stride=4), :])

    ones = jnp.ones((slab, HEAD_DIM), BF16)
    for slot, src in enumerate((lambda r: v_ref[r, :], lambda r: x4_ref[2, r, :],
                                lambda r: x16_ref[2, r, :])):
        for c in range(seq // slab):
            r = pl.ds(c * slab, slab)
            va_ref[slot, r, 0:HEAD_DIM] = src(r).astype(BF16)
            va_ref[slot, r, HEAD_DIM:2 * HEAD_DIM] = ones

    row = lax.broadcasted_iota(jnp.int32, (slab, 2 * BAND), 0)
    col = lax.broadcasted_iota(jnp.int32, (slab, 2 * BAND), 1)
    dist = jnp.bitwise_and(row, BAND - 1) + BAND - col
    in_band = (dist >= 0) & (dist <= BAND)
    starts_class = (row < BAND) & (col < BAND)
    distf = dist.astype(F32)
    tile1 = jnp.where(in_band, -(slope * (1 * LOG2E)) * distf, NEG_INF)
    bias_ref[1] = tile1
    bias_ref[0] = jnp.where(starts_class, NEG_INF, tile1)
    bias_ref[2] = jnp.where(starts_class | ~in_band, NEG_INF, -(slope * (4 * LOG2E)) * distf)
    dist_own = dist[:, BAND:]
    bias_ref[3, :, 0:BAND] = jnp.where(dist_own >= 0,
                                       -(slope * (16 * LOG2E)) * dist_own.astype(F32), NEG_INF)

    def band_pass(qf, kf, vf, two, first_tile, rest_tile, store):
        kw = 2 * BAND if two else BAND

        if two:
            s_ref[0:BAND, 0:BAND] = jnp.zeros((BAND, BAND), F32)
            s_ref[0:BAND, BAND:kw] = _dot_nt(qf(0, BAND).astype(BF16), kf(0, BAND).astype(BF16))

            def scores(j, carry):
                r = pl.multiple_of(j * BAND, BAND)
                rp = pl.multiple_of(r - BAND, BAND)
                s_ref[pl.ds(r, BAND), :] = _dot_nt(qf(r, BAND).astype(BF16),
                                                   kf(rp, kw).astype(BF16))
                return carry

            lax.fori_loop(1, nblk, scores, 0, unroll=ATTN_UNROLL)
        else:
            def scores(j, carry):
                r = pl.multiple_of(j * BAND, BAND)
                s_ref[pl.ds(r, BAND), 0:kw] = _dot_nt(qf(r, BAND).astype(BF16),
                                                      kf(r, BAND).astype(BF16))
                return carry

            lax.fori_loop(0, nblk, scores, 0, unroll=ATTN_UNROLL)

        def softmax_slab(r, tile):
            s = s_ref[pl.ds(r, slab), 0:kw] * (ATTN_SCALE * LOG2E) + bias_ref[tile, :, 0:kw]
            m = jnp.max(s, axis=-1, keepdims=True)
            p = jnp.exp2(s - m)
            p_ref[pl.ds(r, slab), 0:kw] = p.astype(BF16)
            ls_ref[pl.ds(r, slab), :] = jnp.broadcast_to(m * LN2, (slab, HEAD_DIM))

        first_rest = 0
        if first_tile != rest_tile:
            softmax_slab(0, first_tile)
            first_rest = 1

        def rest_slab(g, carry):
            softmax_slab(pl.multiple_of(g * slab, slab), rest_tile)
            return carry

        lax.fori_loop(first_rest, nblk // ATTN_GROUP, rest_slab, 0)

        def finish(j, r, o_l):
            l = o_l[:, HEAD_DIM:]
            store(j, o_l[:, 0:HEAD_DIM] / l, ls_ref[pl.ds(r, BAND), :] + jnp.log(l))

        if two:
            finish(0, 0, _dot(p_ref[0:BAND, BAND:kw], vf(0, BAND)))

            def values(j, carry):
                r = pl.multiple_of(j * BAND, BAND)
                rp = pl.multiple_of(r - BAND, BAND)
                finish(j, r, _dot(p_ref[pl.ds(r, BAND), :], vf(rp, kw)))
                return carry

            lax.fori_loop(1, nblk, values, 0, unroll=ATTN_UNROLL)
        else:
            def values(j, carry):
                r = pl.multiple_of(j * BAND, BAND)
                finish(j, r, _dot(p_ref[pl.ds(r, BAND), 0:kw], vf(r, BAND)))
                return carry

            lax.fori_loop(0, nblk, values, 0, unroll=ATTN_UNROLL)

    def store_contiguous(slot):
        def store(j, o, lse):
            og_ref[slot, pl.ds(j * BAND, BAND), :] = o
            lse_ref[slot, pl.ds(j * BAND, BAND), :] = lse
        return store

    def store_16_as_4(j, o, lse):
        start = jnp.bitwise_and(j, 3) * part4 + jnp.right_shift(j, 2)
        og_ref[2, pl.ds(start, BAND, stride=4), :] = o
        lse_ref[2, pl.ds(start, BAND, stride=4), :] = lse

    values_of = lambda slot: (lambda start, size: va_ref[slot, pl.ds(start, size), :])
    natural = [lambda start, size, ref=ref: ref[pl.ds(start, size), :] for ref in nat[:2]]
    by4 = [lambda start, size, t=t: x4_ref[t, pl.ds(start, size), :] for t in range(2)]
    by16 = [lambda start, size, t=t: x16_ref[t, pl.ds(start, size), :] for t in range(2)]
    natural.append(values_of(0))
    by4.append(values_of(1))
    by16.append(values_of(2))
    band_pass(*natural, True, 0, 1, store_contiguous(0))
    band_pass(*by4, True, 2, 2, store_contiguous(1))
    band_pass(*by16, False, 3, 3, store_16_as_4)

    for a in range(4):
        for c in range(part4 // BAND):
            src = pl.ds(a * part4 + c * BAND, BAND)
            dst = pl.ds(a + 4 * c * BAND, BAND, stride=4)
            o, lse = _merge_pair(og_ref[1, src, :], lse_ref[1, src, :],
                                 og_ref[2, src, :], lse_ref[2, src, :])
            og_ref[3, dst, :] = o
            lse_ref[3, dst, :] = lse

    def final_merge(c, carry):
        rows = pl.ds(pl.multiple_of(c * BAND, BAND), BAND)
        o, _ = _merge_pair(og_ref[0, rows, :], lse_ref[0, rows, :],
                           og_ref[3, rows, :], lse_ref[3, rows, :])
        o_ref[rows, :] = o.astype(BF16)
        return carry

    lax.fori_loop(0, nblk, final_merge, 0, unroll=2)


def _attn_prompt(layer, zqb, kv, slopes, batch, seq):
    slab = ATTN_GROUP * BAND
    return pl.pallas_call(
        functools.partial(_attn_prompt_kernel, seq=seq),
        grid_spec=pltpu.PrefetchScalarGridSpec(
            num_scalar_prefetch=1,
            grid=(batch, N_HEADS_A),
            in_specs=[
                pl.BlockSpec(memory_space=pltpu.SMEM),
                pl.BlockSpec((seq, HEAD_DIM), lambda b, h, l: (b, h)),
                pl.BlockSpec((seq, HEAD_DIM), lambda b, h, l: (b, h)),
                pl.BlockSpec((seq, HEAD_DIM), lambda b, h, l: (b, N_HEADS_A + h)),
            ],
            out_specs=pl.BlockSpec((seq, HEAD_DIM), lambda b, h, l: (b, h)),
            scratch_shapes=[
                pltpu.VMEM((3, seq, HEAD_DIM), F32),
                pltpu.VMEM((3, seq, HEAD_DIM), F32),
                pltpu.VMEM((3, seq, 2 * HEAD_DIM), BF16),
                pltpu.VMEM((seq, 2 * BAND), F32),
                pltpu.VMEM((seq, 2 * BAND), BF16),
                pltpu.VMEM((seq, HEAD_DIM), F32),
                pltpu.VMEM((4, slab, 2 * BAND), F32),
                pltpu.VMEM((4, seq, HEAD_DIM), F32),
                pltpu.VMEM((4, seq, HEAD_DIM), F32),
            ],
        ),
        out_shape=jax.ShapeDtypeStruct((batch * seq, WIDTH_A), BF16),
        compiler_params=pltpu.CompilerParams(
            dimension_semantics=("parallel", "parallel"), vmem_limit_bytes=V7X_VMEM_LIMIT),
        name="attn_prompt",
    )(layer, slopes, zqb, kv, kv)


def _group_norm_gate(o, gate, gn_g, gn_b):
    mu = jnp.mean(o, axis=-1, keepdims=True)
    d = o - mu
    var = jnp.mean(d * d, axis=-1, keepdims=True)
    on = d * lax.rsqrt(var + GN_EPS)
    return (on * gn_g + gn_b) * (gate * jax.nn.sigmoid(gate))


def _ret_prompt_kernel(l_ref, lg_ref, q_ref, k_ref, v_ref, gate_ref, gng_ref, gnb_ref, o_ref,
                       s_ref, st_ref, *, seq):
    pair = pl.program_id(1)
    c_len = RET_CHUNK
    lane = lax.broadcasted_iota(jnp.int32, (c_len, 2 * DK_B), 1)
    pos_r = lax.broadcasted_iota(jnp.int32, (c_len, c_len), 0)
    pos_c = lax.broadcasted_iota(jnp.int32, (c_len, c_len), 1)
    rel = (pos_r - pos_c).astype(F32)
    pos = lax.broadcasted_iota(jnp.int32, (c_len, 1), 0).astype(F32)
    ones11 = jnp.ones((1, 1), F32)

    heads = []
    for hx in range(2):
        lg = lg_ref[2 * pair + hx]
        decay_mask = jnp.where(rel >= 0, jnp.exp(jnp.maximum(rel, 0.0) * lg), 0.0)
        q_decay = jnp.exp((pos + 1.0) * lg)
        k_decay = jnp.exp((c_len - 1.0 - pos) * lg)
        chunk_decay = jnp.exp((c_len * ones11) * lg)
        heads.append((decay_mask, q_decay, k_decay, chunk_decay))

    st_ref[...] = jnp.zeros_like(st_ref)

    def chunk(c, carry):
        for bi in range(RET_BATCH):
            r = pl.multiple_of(bi * seq + c * c_len, c_len)
            q2 = q_ref[pl.ds(r, c_len), :]
            k2 = k_ref[pl.ds(r, c_len), :] * KB_SCALE
            kb = k2.astype(BF16)
            for hx in range(2):
                decay_mask, q_decay, k_decay, chunk_decay = heads[hx]
                cols = slice(hx * DV_B, (hx + 1) * DV_B)
                qx = jnp.where((lane >= DK_B) if hx else (lane < DK_B), q2, 0.0).astype(BF16)
                vx = v_ref[pl.ds(r, c_len), cols].astype(BF16)
                state = st_ref[2 * bi + hx]
                inner = _dot_nt(qx, kb) * decay_mask
                o = _dot(inner.astype(BF16), vx) + _dot(qx, state.astype(BF16)) * q_decay
                kd_t = (k2 * k_decay).T.astype(BF16)
                st_ref[2 * bi + hx] = state * chunk_decay + _dot(kd_t, vx)
                res = _group_norm_gate(o, gate_ref[pl.ds(r, c_len), cols],
                                       gng_ref[:, cols], gnb_ref[:, cols])
                o_ref[pl.ds(r, c_len), cols] = res.astype(BF16)
        return carry

    lax.fori_loop(0, seq // c_len, chunk, 0)
    for bi in range(RET_BATCH):
        s_ref[bi, 0] = st_ref[2 * bi, 0:DK_B, :]
        s_ref[bi, 1] = st_ref[2 * bi + 1, DK_B:2 * DK_B, :]


def _ret_prompt(layer, z, log_g, gn_g, gn_b, batch, seq):
    assert batch % RET_BATCH == 0
    rows = RET_BATCH * seq
    qcol = WIDTH_A // (2 * DK_B)
    kcol = qcol + WIDTH_BQK // (2 * DK_B)
    vcol = (WIDTH_A + 2 * WIDTH_BQK) // (2 * DV_B)
    gcol = vcol + WIDTH_BV // (2 * DV_B)
    return pl.pallas_call(
        functools.partial(_ret_prompt_kernel, seq=seq),
        grid_spec=pltpu.PrefetchScalarGridSpec(
            num_scalar_prefetch=1,
            grid=(batch // RET_BATCH, N_HEADS_B // 2),
            in_specs=[
                pl.BlockSpec(memory_space=pltpu.SMEM),
                pl.BlockSpec((rows, 2 * DK_B), lambda b, p, l: (b, qcol + p)),
                pl.BlockSpec((rows, 2 * DK_B), lambda b, p, l: (b, kcol + p)),
                pl.BlockSpec((rows, 2 * DV_B), lambda b, p, l: (b, vcol + p)),
                pl.BlockSpec((rows, 2 * DV_B), lambda b, p, l: (b, gcol + p)),
                pl.BlockSpec((None, 1, 2 * DV_B), lambda b, p, l: (l[0], 0, p)),
                pl.BlockSpec((None, 1, 2 * DV_B), lambda b, p, l: (l[0], 0, p)),
            ],
            out_specs=[
                pl.BlockSpec((rows, 2 * DV_B), lambda b, p, l: (b, p)),
                pl.BlockSpec((RET_BATCH, 2, DK_B, DV_B), lambda b, p, l: (b, p, 0, 0)),
            ],
            scratch_shapes=[pltpu.VMEM((2 * RET_BATCH, 2 * DK_B, DV_B), F32)],
        ),
        out_shape=[jax.ShapeDtypeStruct((batch * seq, WIDTH_BV), BF16),
                   jax.ShapeDtypeStruct((batch, N_HEADS_B, DK_B, DV_B), F32)],
        compiler_params=pltpu.CompilerParams(
            dimension_semantics=("parallel", "parallel"), vmem_limit_bytes=V7X_VMEM_LIMIT),
        name="ret_prompt",
    )(layer, log_g, z, z, z, z, gn_g, gn_b)


def _attn_decode_kernel(l_ref, slope_ref, z_ref, k1_ref, k4_ref, k16_ref, v1_ref, v4_ref, v16_ref,
                        o_ref):
    row = z_ref[0]
    steps = (BAND - lax.broadcasted_iota(jnp.int32, (BAND, 1), 0)).astype(F32)
    views = ((k1_ref, v1_ref, 1), (k4_ref, v4_ref, 4), (k16_ref, v16_ref, 16))
    for h in range(N_HEADS_A):
        cols = slice(h * HEAD_DIM, (h + 1) * HEAD_DIM)
        qh = row[:, cols]
        k_new = row[:, WIDTH_A + h * HEAD_DIM:WIDTH_A + (h + 1) * HEAD_DIM]
        v_new = row[:, 2 * WIDTH_A + h * HEAD_DIM:2 * WIDTH_A + (h + 1) * HEAD_DIM]
        slope = slope_ref[h]
        s_new = jnp.sum(qh * k_new, axis=1, keepdims=True) * ATTN_SCALE
        outs, lses = [], []
        for k_ref, v_ref, dil in views:
            s = (jnp.sum(k_ref[:, h, :] * qh, axis=1, keepdims=True) * ATTN_SCALE
                 - slope * (steps * dil))
            m = jnp.maximum(jnp.max(s, axis=0, keepdims=True), s_new)
            e = jnp.exp(s - m)
            e_new = jnp.exp(s_new - m)
            l = jnp.sum(e, axis=0, keepdims=True) + e_new
            o = (jnp.sum(e * v_ref[:, h, :], axis=0, keepdims=True) + e_new * v_new) / l
            outs.append(o)
            lses.append(m + jnp.log(l))
        m = jnp.maximum(jnp.maximum(lses[0], lses[1]), lses[2])
        w = [jnp.exp(x - m) for x in lses]
        o = (w[0] * outs[0] + w[1] * outs[1] + w[2] * outs[2]) / (w[0] + w[1] + w[2])
        o_ref[0, :, cols] = o.astype(BF16)


def _attn_decode(layer, slopes, z3, cache_k, cache_v):
    depth, nb, n_past = cache_k.shape[0], cache_k.shape[1], cache_k.shape[2]
    specs, args = [], []
    for cache in (cache_k, cache_v):
        for _, dil in DILATIONS:
            view = cache.reshape(depth, nb, n_past // dil, dil, N_HEADS_A, HEAD_DIM)
            last = n_past // dil // BAND - 1
            specs.append(pl.BlockSpec((None, None, BAND, None, N_HEADS_A, HEAD_DIM),
                                      lambda b, l, last=last: (l[0], b, last, 0, 0, 0)))
            args.append(view)
    return pl.pallas_call(
        _attn_decode_kernel,
        grid_spec=pltpu.PrefetchScalarGridSpec(
            num_scalar_prefetch=1,
            grid=(nb,),
            in_specs=[
                pl.BlockSpec(memory_space=pltpu.SMEM),
                pl.BlockSpec((1, 1, 3 * WIDTH_A), lambda b, l: (b, 0, 0)),
            ] + specs,
            out_specs=pl.BlockSpec((1, 1, WIDTH_A), lambda b, l: (b, 0, 0)),
        ),
        out_shape=jax.ShapeDtypeStruct((nb, 1, WIDTH_A), BF16),
        compiler_params=pltpu.CompilerParams(
            dimension_semantics=("parallel",), vmem_limit_bytes=V7X_VMEM_LIMIT),
        name="attn_decode",
    )(layer, slopes, z3, *args)


def _ret_decode_kernel(l_ref, lg_ref, z_ref, st_ref, gng_ref, gnb_ref, o_ref, sn_ref):
    row = z_ref[0]
    eye = (lax.broadcasted_iota(jnp.int32, (DK_B, DK_B), 0)
           == lax.broadcasted_iota(jnp.int32, (DK_B, DK_B), 1))
    ones11 = jnp.ones((1, 1), F32)
    v0 = 2 * WIDTH_BQK
    g0 = v0 + WIDTH_BV
    for h in range(N_HEADS_B):
        qh = row[:, h * DK_B:(h + 1) * DK_B]
        kh = row[:, WIDTH_BQK + h * DK_B:WIDTH_BQK + (h + 1) * DK_B] * KB_SCALE
        vh = row[:, v0 + h * DV_B:v0 + (h + 1) * DV_B]
        gh = row[:, g0 + h * DV_B:g0 + (h + 1) * DV_B]
        cols = slice(h * DV_B, (h + 1) * DV_B)
        gamma = jnp.exp(ones11 * lg_ref[h])
        q_col = jnp.sum(jnp.where(eye, qh, 0.0), axis=1, keepdims=True)
        k_col = jnp.sum(jnp.where(eye, kh, 0.0), axis=1, keepdims=True)
        state = st_ref[h]
        qk = jnp.sum(qh * kh, axis=1, keepdims=True)
        o = qk * vh + jnp.sum(q_col * state, axis=0, keepdims=True) * gamma
        sn_ref[h] = state * gamma + k_col * vh
        res = _group_norm_gate(o, gh, gng_ref[:, cols], gnb_ref[:, cols])
        o_ref[0, :, cols] = res.astype(BF16)


def _ret_decode(layer, log_g, z3, state_ret, gn_g, gn_b):
    nb = state_ret.shape[1]
    width = 2 * WIDTH_BQK + 2 * WIDTH_BV
    return pl.pallas_call(
        _ret_decode_kernel,
        grid_spec=pltpu.PrefetchScalarGridSpec(
            num_scalar_prefetch=1,
            grid=(nb,),
            in_specs=[
                pl.BlockSpec(memory_space=pltpu.SMEM),
                pl.BlockSpec((1, 1, width), lambda b, l: (b, 0, 1)),
                pl.BlockSpec((None, None, N_HEADS_B, DK_B, DV_B), lambda b, l: (l[0], b, 0, 0, 0)),
                pl.BlockSpec((None, 1, WIDTH_BV), lambda b, l: (l[0], 0, 0)),
                pl.BlockSpec((None, 1, WIDTH_BV), lambda b, l: (l[0], 0, 0)),
            ],
            out_specs=[
                pl.BlockSpec((1, 1, WIDTH_BV), lambda b, l: (b, 0, 0)),
                pl.BlockSpec((None, N_HEADS_B, DK_B, DV_B), lambda b, l: (b, 0, 0, 0)),
            ],
        ),
        out_shape=[jax.ShapeDtypeStruct((nb, 1, WIDTH_BV), BF16),
                   jax.ShapeDtypeStruct((nb, N_HEADS_B, DK_B, DV_B), F32)],
        compiler_params=pltpu.CompilerParams(
            dimension_semantics=("parallel",), vmem_limit_bytes=V7X_VMEM_LIMIT),
        name="ret_decode",
    )(layer, log_g, z3, state_ret, gn_g, gn_b)


def kernel(x_prompt, x_sample, cache_k, cache_v, state_ret, p_prompt, p_sample, w_in, w_out, gn_g,
           gn_b, ffn1_w1, ffn1_w3, ffn1_w2, ffn2_w1, ffn2_w3, ffn2_w2, w_ple, w_gate, ln_g, ln_b):
    batch, seq, _ = x_prompt.shape
    dec_batch, dec_seq, _ = x_sample.shape
    assert dec_seq == 1 and seq == BAND * DILATIONS[-1][1]
    assert cache_k.shape[2] == BAND * DILATIONS[-1][1]
    m_p, m_s = batch * seq, dec_batch * dec_seq
    tm_p = 1024
    assert m_p % tm_p == 0

    f1 = (ffn1_w1, ffn1_w3, ffn1_w2)
    f2 = (ffn2_w1, ffn2_w3, ffn2_w2)
    w_in_b = w_in.astype(BF16)
    ln_g4 = ln_g.reshape(DEPTH, 4, 1, D_MODEL)
    ln_b4 = ln_b.reshape(DEPTH, 4, 1, D_MODEL)
    gn_g3 = gn_g.reshape(DEPTH, 1, WIDTH_BV)
    gn_b3 = gn_b.reshape(DEPTH, 1, WIDTH_BV)
    p_p = p_prompt.reshape(DEPTH, m_p, PLE_DIM)
    p_s = p_sample.reshape(DEPTH, m_s, PLE_DIM)

    heads_a = jnp.arange(1, N_HEADS_A + 1, dtype=F32)
    slopes = jnp.exp2(-8.0 * heads_a / N_HEADS_A)
    log_g = jnp.log(1.0 - jnp.exp2(-5.0 - jnp.arange(N_HEADS_B, dtype=F32)))

    def layer_step(carry, i):
        xp, xs, k_p, v_p, f1_b = carry
        layer = i.reshape(1)
        nxt = jnp.minimum(i + 1, DEPTH - 1)

        xp, xs, *f2_b = _ffn_ln(jnp.stack([i, i]), xp, xs, *f1_b, *f2, ln_g4, ln_b4, 0, tm_p)

        zp = _in_proj(layer, xp, w_in_b, tm_p, WIDTH_A, 4, skip_after_first=2)
        k_p, v_p, kv_p = _in_proj_kv(layer, xp, w_in_b, k_p, v_p, 512)
        zs = _in_proj(layer, xs, w_in_b, m_s, WIDTH_A, IN_WIDTH // WIDTH_A)

        oa_p = _attn_prompt(layer, zp, kv_p, slopes, batch, seq)
        ob_p, ret_p = _ret_prompt(layer, zp, log_g, gn_g3, gn_b3, batch, seq)
        zs3 = zs.reshape(m_s, 1, IN_WIDTH)
        oa_s = _attn_decode(layer, slopes, zs3, cache_k, cache_v).reshape(m_s, WIDTH_A)
        ob_s, ret_s = _ret_decode(layer, log_g, zs3, state_ret, gn_g3, gn_b3)
        ob_s = ob_s.reshape(m_s, WIDTH_BV)

        xp = _out_proj_ln(layer, oa_p, ob_p, xp, w_out, ln_g4, ln_b4, 512)
        xs = _out_proj_ln(layer, oa_s, ob_s, xs, w_out, ln_g4, ln_b4, m_s)

        xp, xs, *f1_b = _ffn_ln(jnp.stack([i, nxt]), xp, xs, *f2_b, *f1, ln_g4, ln_b4, 2, tm_p)

        xp = _ple_ln(layer, xp, p_p, w_ple, w_gate, ln_g4, ln_b4, 512)
        xs = _ple_ln(layer, xs, p_s, w_ple, w_gate, ln_g4, ln_b4, m_s)

        k_s = zs[:, WIDTH_A:2 * WIDTH_A].reshape(dec_batch, dec_seq, N_HEADS_A, HEAD_DIM)
        v_s = zs[:, 2 * WIDTH_A:3 * WIDTH_A].reshape(dec_batch, dec_seq, N_HEADS_A, HEAD_DIM)
        return (xp, xs, k_p, v_p, tuple(f1_b)), (ret_p, k_s, v_s, ret_s)

    kv_shape = (DEPTH, m_p, N_HEADS_A, HEAD_DIM)
    (xp, xs, k_p, v_p, _), (ret_p, k_s, v_s, ret_s) = lax.scan(
        layer_step,
        (x_prompt.reshape(m_p, D_MODEL), x_sample.reshape(m_s, D_MODEL),
         jnp.zeros(kv_shape, F32), jnp.zeros(kv_shape, F32),
         tuple(w[0].astype(BF16) for w in f1)),
        jnp.arange(DEPTH, dtype=jnp.int32))
    kv_out = (DEPTH, batch, seq, N_HEADS_A, HEAD_DIM)
    return (xp.reshape(batch, seq, D_MODEL), xs.reshape(dec_batch, dec_seq, D_MODEL),
            k_p.reshape(kv_out), v_p.reshape(kv_out), ret_p, k_s, v_s, ret_s)
```

```python
import functools

import jax
import jax.numpy as jnp
from jax import lax
from jax.experimental import pallas as pl
from jax.experimental.pallas import tpu as pltpu

F32 = jnp.float32
BF16 = jnp.bfloat16

D_MODEL = 2048
DEPTH = 4
HEAD_DIM = 128
WIDTH_A = 1024
N_HEADS_A = 8
DILATIONS = ((128, 1), (512, 4), (2048, 16))
N_HEADS_B = 8
DV_B = 128
DK_B = 64
WIDTH_BQK = N_HEADS_B * DK_B
WIDTH_BV = N_HEADS_B * DV_B
IN_WIDTH = 3 * WIDTH_A + 2 * WIDTH_BQK + 2 * WIDTH_BV
FFN_DIM = 5632
PLE_DIM = 256
RET_CHUNK = 128
LN_EPS = 1e-5
GN_EPS = 1e-5
NEG_INF = -1e30
DEEPNORM_ALPHA = (2 * DEPTH) ** 0.25
ATTN_SCALE = HEAD_DIM ** -0.5
KB_SCALE = DK_B ** -0.5

BAND = 128
V7X_VMEM_LIMIT = 56 * 1024 * 1024
V7X_VMEM_LIMIT_FFN = 62 * 1024 * 1024
FFN_TILE_F = 512
LN_ROWS = 128
ATTN_UNROLL = 4
ATTN_GROUP = 4
RET_BATCH = 2
LOG2E = 1.4426950408889634
LN2 = 0.6931471805599453


def _dot(a, b):
    return jnp.dot(a, b, preferred_element_type=F32)


def _dot_nt(a, b):
    return lax.dot_general(a, b, (((1,), (1,)), ((), ())), preferred_element_type=F32)


def _layer_norm_rows(y, g, b):
    mu = jnp.mean(y, axis=-1, keepdims=True)
    d = y - mu
    var = jnp.mean(d * d, axis=-1, keepdims=True)
    return d * lax.rsqrt(var + LN_EPS) * g + b


def _residual_ln_epilogue(x_ref, o_ref, g_ref, b_ref, branch_scale):
    rows = o_ref.shape[0]
    step = min(rows, LN_ROWS)
    g = g_ref[...]
    b = b_ref[...]

    def body(c, carry):
        r = pl.multiple_of(c * step, step)
        y = DEEPNORM_ALPHA * x_ref[pl.ds(r, step), :] + branch_scale * o_ref[pl.ds(r, step), :]
        o_ref[pl.ds(r, step), :] = _layer_norm_rows(y, g, b)
        return carry

    lax.fori_loop(0, rows // step, body, 0)


def _swiglu_step(xb, w1b, w3b, w2b):
    h1 = _dot(xb, w1b)
    h3 = _dot(xb, w3b)
    act = (h1 * jax.nn.sigmoid(h1) * h3).astype(BF16)
    return _dot(act, w2b)


def _ffn_kernel(l_ref, x_ref, xs_ref, w1_ref, w3_ref, w2_ref, c1_ref, c3_ref, c2_ref, g_ref, b_ref,
                o_ref, os_ref, n1_ref, n3_ref, n2_ref, xb_ref, xsb_ref, *, nf):
    i = pl.program_id(0)
    f = pl.program_id(1)

    @pl.when(f == 0)
    def _():
        xb_ref[...] = x_ref[...].astype(BF16)
        o_ref[...] = jnp.zeros_like(o_ref)

    @pl.when((f == 0) & (i == 0))
    def _():
        xsb_ref[...] = xs_ref[...].astype(BF16)
        os_ref[...] = jnp.zeros_like(os_ref)

    n1_ref[...] = c1_ref[...].astype(BF16)
    n3_ref[...] = c3_ref[...].astype(BF16)
    n2_ref[...] = c2_ref[...].astype(BF16)

    o_ref[...] += _swiglu_step(xb_ref[...], w1_ref[...], w3_ref[...], w2_ref[...])

    @pl.when(i == 0)
    def _():
        os_ref[...] += _swiglu_step(xsb_ref[...], w1_ref[...], w3_ref[...], w2_ref[...])

    @pl.when(f == nf - 1)
    def _():
        _residual_ln_epilogue(x_ref, o_ref, g_ref, b_ref, 0.5)

    @pl.when((f == nf - 1) & (i == 0))
    def _():
        _residual_ln_epilogue(xs_ref, os_ref, g_ref, b_ref, 0.5)


def _ffn_ln(layers, x, xs, w1, w3, w2, c1, c3, c2, ln_g, ln_b, ln_idx, tm):
    m, ms = x.shape[0], xs.shape[0]
    nf = FFN_DIM // FFN_TILE_F
    cr = D_MODEL // (m // tm)
    return pl.pallas_call(
        functools.partial(_ffn_kernel, nf=nf),
        grid_spec=pltpu.PrefetchScalarGridSpec(
            num_scalar_prefetch=1,
            grid=(m // tm, nf),
            in_specs=[
                pl.BlockSpec((tm, D_MODEL), lambda i, f, l: (i, 0)),
                pl.BlockSpec((ms, D_MODEL), lambda i, f, l: (0, 0)),
                pl.BlockSpec((D_MODEL, FFN_TILE_F), lambda i, f, l: (0, f)),
                pl.BlockSpec((D_MODEL, FFN_TILE_F), lambda i, f, l: (0, f)),
                pl.BlockSpec((FFN_TILE_F, D_MODEL), lambda i, f, l: (f, 0)),
                pl.BlockSpec((None, cr, FFN_TILE_F), lambda i, f, l: (l[1], i, f)),
                pl.BlockSpec((None, cr, FFN_TILE_F), lambda i, f, l: (l[1], i, f)),
                pl.BlockSpec((None, FFN_TILE_F, cr), lambda i, f, l: (l[1], f, i)),
                pl.BlockSpec((None, None, 1, D_MODEL), lambda i, f, l: (l[0], ln_idx, 0, 0)),
                pl.BlockSpec((None, None, 1, D_MODEL), lambda i, f, l: (l[0], ln_idx, 0, 0)),
            ],
            out_specs=[
                pl.BlockSpec((tm, D_MODEL), lambda i, f, l: (i, 0)),
                pl.BlockSpec((ms, D_MODEL), lambda i, f, l: (0, 0)),
                pl.BlockSpec((cr, FFN_TILE_F), lambda i, f, l: (i, f)),
                pl.BlockSpec((cr, FFN_TILE_F), lambda i, f, l: (i, f)),
                pl.BlockSpec((FFN_TILE_F, cr), lambda i, f, l: (f, i)),
            ],
            scratch_shapes=[pltpu.VMEM((tm, D_MODEL), BF16), pltpu.VMEM((ms, D_MODEL), BF16)],
        ),
        out_shape=[jax.ShapeDtypeStruct((m, D_MODEL), F32),
                   jax.ShapeDtypeStruct((ms, D_MODEL), F32),
                   jax.ShapeDtypeStruct((D_MODEL, FFN_DIM), BF16),
                   jax.ShapeDtypeStruct((D_MODEL, FFN_DIM), BF16),
                   jax.ShapeDtypeStruct((FFN_DIM, D_MODEL), BF16)],
        compiler_params=pltpu.CompilerParams(
            dimension_semantics=("arbitrary", "arbitrary"), vmem_limit_bytes=V7X_VMEM_LIMIT_FFN),
        name="ffn_ln",
    )(layers, x, xs, w1, w3, w2, c1, c3, c2, ln_g, ln_b)


def _proj_kernel(l_ref, x_ref, w_ref, o_ref, xb_ref):
    @pl.when(pl.program_id(1) == 0)
    def _():
        xb_ref[...] = x_ref[...].astype(BF16)

    o_ref[...] = _dot(xb_ref[...], w_ref[...])


def _in_proj(layer, x, w_in, tm, tn, n_blocks, skip_after_first=0):
    m = x.shape[0]
    col = lambda n: n + jnp.where(n > 0, skip_after_first, 0)
    return pl.pallas_call(
        _proj_kernel,
        grid_spec=pltpu.PrefetchScalarGridSpec(
            num_scalar_prefetch=1,
            grid=(m // tm, n_blocks),
            in_specs=[
                pl.BlockSpec((tm, D_MODEL), lambda i, n, l: (i, 0)),
                pl.BlockSpec((None, D_MODEL, tn), lambda i, n, l: (l[0], 0, col(n))),
            ],
            out_specs=pl.BlockSpec((tm, tn), lambda i, n, l: (i, n)),
            scratch_shapes=[pltpu.VMEM((tm, D_MODEL), BF16)],
        ),
        out_shape=jax.ShapeDtypeStruct((m, n_blocks * tn), F32),
        compiler_params=pltpu.CompilerParams(
            dimension_semantics=("parallel", "arbitrary"), vmem_limit_bytes=V7X_VMEM_LIMIT),
        name="in_proj",
    )(layer, x, w_in)


def _proj_kv_kernel(l_ref, x_ref, w_ref, k_in_ref, v_in_ref, k_ref, v_ref, kv_ref):
    xb = x_ref[...].astype(BF16)
    for n, dst_ref in enumerate((k_ref, v_ref)):
        kv_ref[:, n * WIDTH_A:(n + 1) * WIDTH_A] = _dot(xb, w_ref[:, n * WIDTH_A:(n + 1) * WIDTH_A])
        for h in range(N_HEADS_A):
            c0 = n * WIDTH_A + h * HEAD_DIM
            dst_ref[:, h, :] = kv_ref[:, c0:c0 + HEAD_DIM]


def _in_proj_kv(layer, x, w_kv, k_all, v_all, tm):
    m = x.shape[0]
    out_spec = pl.BlockSpec((None, tm, N_HEADS_A, HEAD_DIM), lambda i, l: (l[0], i, 0, 0))
    return pl.pallas_call(
        _proj_kv_kernel,
        grid_spec=pltpu.PrefetchScalarGridSpec(
            num_scalar_prefetch=1,
            grid=(m // tm,),
            in_specs=[
                pl.BlockSpec((tm, D_MODEL), lambda i, l: (i, 0)),
                pl.BlockSpec((None, D_MODEL, 2 * WIDTH_A), lambda i, l: (l[0], 0, 0)),
                pl.BlockSpec(memory_space=pl.ANY),
                pl.BlockSpec(memory_space=pl.ANY),
            ],
            out_specs=[out_spec, out_spec, pl.BlockSpec((tm, 2 * WIDTH_A), lambda i, l: (i, 0))],
        ),
        out_shape=[jax.ShapeDtypeStruct(k_all.shape, F32), jax.ShapeDtypeStruct(v_all.shape, F32),
                   jax.ShapeDtypeStruct((m, 2 * WIDTH_A), F32)],
        input_output_aliases={3: 0, 4: 1},
        compiler_params=pltpu.CompilerParams(
            dimension_semantics=("parallel",), vmem_limit_bytes=V7X_VMEM_LIMIT),
        name="in_proj_kv",
    )(layer, x, w_kv, k_all, v_all)


def _out_proj_kernel(l_ref, oa_ref, ob_ref, x_ref, w_ref, g_ref, b_ref, o_ref, wb_ref):
    @pl.when(pl.program_id(0) == 0)
    def _():
        wb_ref[...] = w_ref[...].astype(BF16)

    o_ref[...] = _dot(oa_ref[...], wb_ref[0:WIDTH_A, :]) + _dot(ob_ref[...], wb_ref[WIDTH_A:, :])
    _residual_ln_epilogue(x_ref, o_ref, g_ref, b_ref, 1.0)


def _out_proj_ln(layer, oa, ob, x, w_out, ln_g, ln_b, tm):
    m = x.shape[0]
    return pl.pallas_call(
        _out_proj_kernel,
        grid_spec=pltpu.PrefetchScalarGridSpec(
            num_scalar_prefetch=1,
            grid=(m // tm,),
            in_specs=[
                pl.BlockSpec((tm, WIDTH_A), lambda i, l: (i, 0)),
                pl.BlockSpec((tm, WIDTH_BV), lambda i, l: (i, 0)),
                pl.BlockSpec((tm, D_MODEL), lambda i, l: (i, 0)),
                pl.BlockSpec((None, WIDTH_A + WIDTH_BV, D_MODEL), lambda i, l: (l[0], 0, 0),
                             pipeline_mode=pl.Buffered(1)),
                pl.BlockSpec((None, None, 1, D_MODEL), lambda i, l: (l[0], 1, 0, 0)),
                pl.BlockSpec((None, None, 1, D_MODEL), lambda i, l: (l[0], 1, 0, 0)),
            ],
            out_specs=pl.BlockSpec((tm, D_MODEL), lambda i, l: (i, 0)),
            scratch_shapes=[pltpu.VMEM((WIDTH_A + WIDTH_BV, D_MODEL), BF16)],
        ),
        out_shape=jax.ShapeDtypeStruct((m, D_MODEL), F32),
        compiler_params=pltpu.CompilerParams(
            dimension_semantics=("arbitrary",), vmem_limit_bytes=V7X_VMEM_LIMIT),
        name="out_proj_ln",
    )(layer, oa, ob, x, w_out, ln_g, ln_b)


def _ple_kernel(l_ref, x_ref, p_ref, wp_ref, wg_ref, g_ref, b_ref, o_ref, wpb_ref, wgb_ref):
    @pl.when(pl.program_id(0) == 0)
    def _():
        wpb_ref[...] = wp_ref[...].astype(BF16)
        wgb_ref[...] = wg_ref[...].astype(BF16)

    gate = jax.nn.sigmoid(_dot(x_ref[...].astype(BF16), wgb_ref[...]))
    o_ref[...] = _dot(p_ref[...].astype(BF16), wpb_ref[...]) * gate
    _residual_ln_epilogue(x_ref, o_ref, g_ref, b_ref, 1.0)


def _ple_ln(layer, x, p, w_ple, w_gate, ln_g, ln_b, tm):
    m = x.shape[0]
    return pl.pallas_call(
        _ple_kernel,
        grid_spec=pltpu.PrefetchScalarGridSpec(
            num_scalar_prefetch=1,
            grid=(m // tm,),
            in_specs=[
                pl.BlockSpec((tm, D_MODEL), lambda i, l: (i, 0)),
                pl.BlockSpec((None, tm, PLE_DIM), lambda i, l: (l[0], i, 0)),
                pl.BlockSpec((None, PLE_DIM, D_MODEL), lambda i, l: (l[0], 0, 0),
                             pipeline_mode=pl.Buffered(1)),
                pl.BlockSpec((None, D_MODEL, D_MODEL), lambda i, l: (l[0], 0, 0),
                             pipeline_mode=pl.Buffered(1)),
                pl.BlockSpec((None, None, 1, D_MODEL), lambda i, l: (l[0], 3, 0, 0)),
                pl.BlockSpec((None, None, 1, D_MODEL), lambda i, l: (l[0], 3, 0, 0)),
            ],
            out_specs=pl.BlockSpec((tm, D_MODEL), lambda i, l: (i, 0)),
            scratch_shapes=[pltpu.VMEM((PLE_DIM, D_MODEL), BF16),
                            pltpu.VMEM((D_MODEL, D_MODEL), BF16)],
        ),
        out_shape=jax.ShapeDtypeStruct((m, D_MODEL), F32),
        compiler_params=pltpu.CompilerParams(
            dimension_semantics=("arbitrary",), vmem_limit_bytes=V7X_VMEM_LIMIT),
        name="ple_ln",
    )(layer, x, p, w_ple, w_gate, ln_g, ln_b)


def _merge_pair(o_a, l_a, o_b, l_b):
    m = jnp.maximum(l_a, l_b)
    e_a = jnp.exp(l_a - m)
    e_b = jnp.exp(l_b - m)
    den = e_a + e_b
    return (e_a * o_a + e_b * o_b) / den, m + jnp.log(den)


def _attn_prompt_kernel(l_ref, slope_ref, q_ref, k_ref, v_ref, o_ref,
                        x4_ref, x16_ref, va_ref, s_ref, p_ref, ls_ref, bias_ref, og_ref, lse_ref,
                        *, seq):
    nblk = seq // BAND
    slab = ATTN_GROUP * BAND
    part4 = seq // 4
    slope = slope_ref[pl.program_id(1)]

    nat = (q_ref, k_ref, v_ref)
    for t in range(3):
        for a in range(4):
            for c in range(part4 // BAND):
                x4_ref[t, pl.ds(a * part4 + c * BAND, BAND), :] = (
                    nat[t][pl.ds(a + 4 * c * BAND, BAND, stride=4), :])
    for t in range(3):
        for a in range(4):
            for b in range(4):
                x16_ref[t, pl.ds((a + 4 * b) * BAND, BAND), :] = (
                    x4_ref[t, pl.ds(a * part4 + b, BAND, stride=4), :])

    ones = jnp.ones((slab, HEAD_DIM), BF16)
    for slot, src in enumerate((lambda r: v_ref[r, :], lambda r: x4_ref[2, r, :],
                                lambda r: x16_ref[2, r, :])):
        for c in range(seq // slab):
            r = pl.ds(c * slab, slab)
            va_ref[slot, r, 0:HEAD_DIM] = src(r).astype(BF16)
            va_ref[slot, r, HEAD_DIM:2 * HEAD_DIM] = ones

    row = lax.broadcasted_iota(jnp.int32, (slab, 2 * BAND), 0)
    col = lax.broadcasted_iota(jnp.int32, (slab, 2 * BAND), 1)
    dist = jnp.bitwise_and(row, BAND - 1) + BAND - col
    in_band = (dist >= 0) & (dist <= BAND)
    starts_class = (row < BAND) & (col < BAND)
    distf = dist.astype(F32)
    tile1 = jnp.where(in_band, -(slope * (1 * LOG2E)) * distf, NEG_INF)
    bias_ref[1] = tile1
    bias_ref[0] = jnp.where(starts_class, NEG_INF, tile1)
    bias_ref[2] = jnp.where(starts_class | ~in_band, NEG_INF, -(slope * (4 * LOG2E)) * distf)
    dist_own = dist[:, BAND:]
    bias_ref[3, :, 0:BAND] = jnp.where(dist_own >= 0,
                                       -(slope * (16 * LOG2E)) * dist_own.astype(F32), NEG_INF)

    def band_pass(qf, kf, vf, two, first_tile, rest_tile, store):
        kw = 2 * BAND if two else BAND

        if two:
            s_ref[0:BAND, 0:BAND] = jnp.zeros((BAND, BAND), F32)
            s_ref[0:BAND, BAND:kw] = _dot_nt(qf(0, BAND).astype(BF16), kf(0, BAND).astype(BF16))

            def scores(j, carry):
                r = pl.multiple_of(j * BAND, BAND)
                rp = pl.multiple_of(r - BAND, BAND)
                s_ref[pl.ds(r, BAND), :] = _dot_nt(qf(r, BAND).astype(BF16),
                                                   kf(rp, kw).astype(BF16))
                return carry

            lax.fori_loop(1, nblk, scores, 0, unroll=ATTN_UNROLL)
        else:
            def scores(j, carry):
                r = pl.multiple_of(j * BAND, BAND)
                s_ref[pl.ds(r, BAND), 0:kw] = _dot_nt(qf(r, BAND).astype(BF16),
                                                      kf(r, BAND).astype(BF16))
                return carry

            lax.fori_loop(0, nblk, scores, 0, unroll=ATTN_UNROLL)

        def softmax_slab(r, tile):
            s = s_ref[pl.ds(r, slab), 0:kw] * (ATTN_SCALE * LOG2E) + bias_ref[tile, :, 0:kw]
            m = jnp.max(s, axis=-1, keepdims=True)
            p = jnp.exp2(s - m)
            p_ref[pl.ds(r, slab), 0:kw] = p.astype(BF16)
            ls_ref[pl.ds(r, slab), :] = jnp.broadcast_to(m * LN2, (slab, HEAD_DIM))

        first_rest = 0
        if first_tile != rest_tile:
            softmax_slab(0, first_tile)
            first_rest = 1

        def rest_slab(g, carry):
            softmax_slab(pl.multiple_of(g * slab, slab), rest_tile)
            return carry

        lax.fori_loop(first_rest, nblk // ATTN_GROUP, rest_slab, 0)

        def finish(j, r, o_l):
            l = o_l[:, HEAD_DIM:]
            store(j, o_l[:, 0:HEAD_DIM] / l, ls_ref[pl.ds(r, BAND), :] + jnp.log(l))

        if two:
            finish(0, 0, _dot(p_ref[0:BAND, BAND:kw], vf(0, BAND)))

            def values(j, carry):
                r = pl.multiple_of(j * BAND, BAND)
                rp = pl.multiple_of(r - BAND, BAND)
                finish(j, r, _dot(p_ref[pl.ds(r, BAND), :], vf(rp, kw)))
                return carry

            lax.fori_loop(1, nblk, values, 0, unroll=ATTN_UNROLL)
        else:
            def values(j, carry):
                r = pl.multiple_of(j * BAND, BAND)
                finish(j, r, _dot(p_ref[pl.ds(r, BAND), 0:kw], vf(r, BAND)))
                return carry

            lax.fori_loop(0, nblk, values, 0, unroll=ATTN_UNROLL)

    def store_contiguous(slot):
        def store(j, o, lse):
            og_ref[slot, pl.ds(j * BAND, BAND), :] = o
            lse_ref[slot, pl.ds(j * BAND, BAND), :] = lse
        return store

    def store_16_as_4(j, o, lse):
        start = jnp.bitwise_and(j, 3) * part4 + jnp.right_shift(j, 2)
        og_ref[2, pl.ds(start, BAND, stride=4), :] = o
        lse_ref[2, pl.ds(start, BAND, stride=4), :] = lse

    values_of = lambda slot: (lambda start, size: va_ref[slot, pl.ds(start, size), :])
    natural = [lambda start, size, ref=ref: ref[pl.ds(start, size), :] for ref in nat[:2]]
    by4 = [lambda start, size, t=t: x4_ref[t, pl.ds(start, size), :] for t in range(2)]
    by16 = [lambda start, size, t=t: x16_ref[t, pl.ds(start, size), :] for t in range(2)]
    natural.append(values_of(0))
    by4.append(values_of(1))
    by16.append(values_of(2))
    band_pass(*natural, True, 0, 1, store_contiguous(0))
    band_pass(*by4, True, 2, 2, store_contiguous(1))
    band_pass(*by16, False, 3, 3, store_16_as_4)

    for a in range(4):
        for c in range(part4 // BAND):
            src = pl.ds(a * part4 + c * BAND, BAND)
            dst = pl.ds(a + 4 * c * BAND, BAND, stride=4)
            o, lse = _merge_pair(og_ref[1, src, :], lse_ref[1, src, :],
                                 og_ref[2, src, :], lse_ref[2, src, :])
            og_ref[3, dst, :] = o
            lse_ref[3, dst, :] = lse

    def final_merge(c, carry):
        rows = pl.ds(pl.multiple_of(c * BAND, BAND), BAND)
        o, _ = _merge_pair(og_ref[0, rows, :], lse_ref[0, rows, :],
                           og_ref[3, rows, :], lse_ref[3, rows, :])
        o_ref[rows, :] = o.astype(BF16)
        return carry

    lax.fori_loop(0, nblk, final_merge, 0, unroll=2)


def _attn_prompt(layer, zqb, kv, slopes, batch, seq):
    slab = ATTN_GROUP * BAND
    return pl.pallas_call(
        functools.partial(_attn_prompt_kernel, seq=seq),
        grid_spec=pltpu.PrefetchScalarGridSpec(
            num_scalar_prefetch=1,
            grid=(batch, N_HEADS_A),
            in_specs=[
                pl.BlockSpec(memory_space=pltpu.SMEM),
                pl.BlockSpec((seq, HEAD_DIM), lambda b, h, l: (b, h)),
                pl.BlockSpec((seq, HEAD_DIM), lambda b, h, l: (b, h)),
                pl.BlockSpec((seq, HEAD_DIM), lambda b, h, l: (b, N_HEADS_A + h)),
            ],
            out_specs=pl.BlockSpec((seq, HEAD_DIM), lambda b, h, l: (b, h)),
            scratch_shapes=[
                pltpu.VMEM((3, seq, HEAD_DIM), F32),
                pltpu.VMEM((3, seq, HEAD_DIM), F32),
                pltpu.VMEM((3, seq, 2 * HEAD_DIM), BF16),
                pltpu.VMEM((seq, 2 * BAND), F32),
                pltpu.VMEM((seq, 2 * BAND), BF16),
                pltpu.VMEM((seq, HEAD_DIM), F32),
                pltpu.VMEM((4, slab, 2 * BAND), F32),
                pltpu.VMEM((4, seq, HEAD_DIM), F32),
                pltpu.VMEM((4, seq, HEAD_DIM), F32),
            ],
        ),
        out_shape=jax.ShapeDtypeStruct((batch * seq, WIDTH_A), BF16),
        compiler_params=pltpu.CompilerParams(
            dimension_semantics=("parallel", "parallel"), vmem_limit_bytes=V7X_VMEM_LIMIT),
        name="attn_prompt",
    )(layer, slopes, zqb, kv, kv)


def _group_norm_gate(o, gate, gn_g, gn_b):
    mu = jnp.mean(o, axis=-1, keepdims=True)
    d = o - mu
    var = jnp.mean(d * d, axis=-1, keepdims=True)
    on = d * lax.rsqrt(var + GN_EPS)
    return (on * gn_g + gn_b) * (gate * jax.nn.sigmoid(gate))


def _ret_prompt_kernel(l_ref, lg_ref, q_ref, k_ref, v_ref, gate_ref, gng_ref, gnb_ref, o_ref,
                       s_ref, st_ref, *, seq):
    pair = pl.program_id(1)
    c_len = RET_CHUNK
    lane = lax.broadcasted_iota(jnp.int32, (c_len, 2 * DK_B), 1)
    pos_r = lax.broadcasted_iota(jnp.int32, (c_len, c_len), 0)
    pos_c = lax.broadcasted_iota(jnp.int32, (c_len, c_len), 1)
    rel = (pos_r - pos_c).astype(F32)
    pos = lax.broadcasted_iota(jnp.int32, (c_len, 1), 0).astype(F32)
    ones11 = jnp.ones((1, 1), F32)

    heads = []
    for hx in range(2):
        lg = lg_ref[2 * pair + hx]
        decay_mask = jnp.where(rel >= 0, jnp.exp(jnp.maximum(rel, 0.0) * lg), 0.0)
        q_decay = jnp.exp((pos + 1.0) * lg)
        k_decay = jnp.exp((c_len - 1.0 - pos) * lg)
        chunk_decay = jnp.exp((c_len * ones11) * lg)
        heads.append((decay_mask, q_decay, k_decay, chunk_decay))

    st_ref[...] = jnp.zeros_like(st_ref)

    def chunk(c, carry):
        units = []
        for bi in range(RET_BATCH):
            r = pl.multiple_of(bi * seq + c * c_len, c_len)
            q2 = q_ref[pl.ds(r, c_len), :]
            k2 = k_ref[pl.ds(r, c_len), :] * KB_SCALE
            kb = k2.astype(BF16)
            for hx in range(2):
                qx = jnp.where((lane >= DK_B) if hx else (lane < DK_B), q2, 0.0).astype(BF16)
                units.append((2 * bi + hx, hx, r, qx, kb, k2))
        inners = [_dot_nt(qx, kb) * heads[hx][0] for _, hx, _, qx, kb, _ in units]
        vxs = [v_ref[pl.ds(r, c_len), hx * DV_B:(hx + 1) * DV_B].astype(BF16)
               for _, hx, r, _, _, _ in units]
        states = [st_ref[u] for u, *_ in units]
        intras = [_dot(inner.astype(BF16), vx) for inner, vx in zip(inners, vxs)]
        crosses = [_dot(qx, state.astype(BF16)) for (_, _, _, qx, _, _), state in zip(units, states)]
        updates = [_dot((k2 * heads[hx][2]).T.astype(BF16), vx)
                   for (_, hx, _, _, _, k2), vx in zip(units, vxs)]
        for (u, hx, r, _, _, _), intra, cross, update, state in zip(units, intras, crosses, updates,
                                                                  states):
            _, q_decay, _, chunk_decay = heads[hx]
            cols = slice(hx * DV_B, (hx + 1) * DV_B)
            st_ref[u] = state * chunk_decay + update
            res = _group_norm_gate(intra + cross * q_decay, gate_ref[pl.ds(r, c_len), cols],
                                   gng_ref[:, cols], gnb_ref[:, cols])
            o_ref[pl.ds(r, c_len), cols] = res.astype(BF16)
        return carry

    lax.fori_loop(0, seq // c_len, chunk, 0)
    for bi in range(RET_BATCH):
        s_ref[bi, 0] = st_ref[2 * bi, 0:DK_B, :]
        s_ref[bi, 1] = st_ref[2 * bi + 1, DK_B:2 * DK_B, :]


def _ret_prompt(layer, z, log_g, gn_g, gn_b, batch, seq):
    assert batch % RET_BATCH == 0
    rows = RET_BATCH * seq
    qcol = WIDTH_A // (2 * DK_B)
    kcol = qcol + WIDTH_BQK // (2 * DK_B)
    vcol = (WIDTH_A + 2 * WIDTH_BQK) // (2 * DV_B)
    gcol = vcol + WIDTH_BV // (2 * DV_B)
    return pl.pallas_call(
        functools.partial(_ret_prompt_kernel, seq=seq),
        grid_spec=pltpu.PrefetchScalarGridSpec(
            num_scalar_prefetch=1,
            grid=(batch // RET_BATCH, N_HEADS_B // 2),
            in_specs=[
                pl.BlockSpec(memory_space=pltpu.SMEM),
                pl.BlockSpec((rows, 2 * DK_B), lambda b, p, l: (b, qcol + p)),
                pl.BlockSpec((rows, 2 * DK_B), lambda b, p, l: (b, kcol + p)),
                pl.BlockSpec((rows, 2 * DV_B), lambda b, p, l: (b, vcol + p)),
                pl.BlockSpec((rows, 2 * DV_B), lambda b, p, l: (b, gcol + p)),
                pl.BlockSpec((None, 1, 2 * DV_B), lambda b, p, l: (l[0], 0, p)),
                pl.BlockSpec((None, 1, 2 * DV_B), lambda b, p, l: (l[0], 0, p)),
            ],
            out_specs=[
                pl.BlockSpec((rows, 2 * DV_B), lambda b, p, l: (b, p)),
                pl.BlockSpec((RET_BATCH, 2, DK_B, DV_B), lambda b, p, l: (b, p, 0, 0)),
            ],
            scratch_shapes=[pltpu.VMEM((2 * RET_BATCH, 2 * DK_B, DV_B), F32)],
        ),
        out_shape=[jax.ShapeDtypeStruct((batch * seq, WIDTH_BV), BF16),
                   jax.ShapeDtypeStruct((batch, N_HEADS_B, DK_B, DV_B), F32)],
        compiler_params=pltpu.CompilerParams(
            dimension_semantics=("parallel", "parallel"), vmem_limit_bytes=V7X_VMEM_LIMIT),
        name="ret_prompt",
    )(layer, log_g, z, z, z, z, gn_g, gn_b)


def _attn_decode_kernel(l_ref, slope_ref, z_ref, k1_ref, k4_ref, k16_ref, v1_ref, v4_ref, v16_ref,
                        o_ref):
    row = z_ref[0]
    steps = (BAND - lax.broadcasted_iota(jnp.int32, (BAND, 1), 0)).astype(F32)
    views = ((k1_ref, v1_ref, 1), (k4_ref, v4_ref, 4), (k16_ref, v16_ref, 16))
    for h in range(N_HEADS_A):
        cols = slice(h * HEAD_DIM, (h + 1) * HEAD_DIM)
        qh = row[:, cols]
        k_new = row[:, WIDTH_A + h * HEAD_DIM:WIDTH_A + (h + 1) * HEAD_DIM]
        v_new = row[:, 2 * WIDTH_A + h * HEAD_DIM:2 * WIDTH_A + (h + 1) * HEAD_DIM]
        slope = slope_ref[h]
        s_new = jnp.sum(qh * k_new, axis=1, keepdims=True) * ATTN_SCALE
        outs, lses = [], []
        for k_ref, v_ref, dil in views:
            s = (jnp.sum(k_ref[:, h, :] * qh, axis=1, keepdims=True) * ATTN_SCALE
                 - slope * (steps * dil))
            m = jnp.maximum(jnp.max(s, axis=0, keepdims=True), s_new)
            e = jnp.exp(s - m)
            e_new = jnp.exp(s_new - m)
            l = jnp.sum(e, axis=0, keepdims=True) + e_new
            o = (jnp.sum(e * v_ref[:, h, :], axis=0, keepdims=True) + e_new * v_new) / l
            outs.append(o)
            lses.append(m + jnp.log(l))
        m = jnp.maximum(jnp.maximum(lses[0], lses[1]), lses[2])
        w = [jnp.exp(x - m) for x in lses]
        o = (w[0] * outs[0] + w[1] * outs[1] + w[2] * outs[2]) / (w[0] + w[1] + w[2])
        o_ref[0, :, cols] = o.astype(BF16)


def _attn_decode(layer, slopes, z3, cache_k, cache_v):
    depth, nb, n_past = cache_k.shape[0], cache_k.shape[1], cache_k.shape[2]
    specs, args = [], []
    for cache in (cache_k, cache_v):
        for _, dil in DILATIONS:
            view = cache.reshape(depth, nb, n_past // dil, dil, N_HEADS_A, HEAD_DIM)
            last = n_past // dil // BAND - 1
            specs.append(pl.BlockSpec((None, None, BAND, None, N_HEADS_A, HEAD_DIM),
                                      lambda b, l, last=last: (l[0], b, last, 0, 0, 0)))
            args.append(view)
    return pl.pallas_call(
        _attn_decode_kernel,
        grid_spec=pltpu.PrefetchScalarGridSpec(
            num_scalar_prefetch=1,
            grid=(nb,),
            in_specs=[
                pl.BlockSpec(memory_space=pltpu.SMEM),
                pl.BlockSpec((1, 1, 3 * WIDTH_A), lambda b, l: (b, 0, 0)),
            ] + specs,
            out_specs=pl.BlockSpec((1, 1, WIDTH_A), lambda b, l: (b, 0, 0)),
        ),
        out_shape=jax.ShapeDtypeStruct((nb, 1, WIDTH_A), BF16),
        compiler_params=pltpu.CompilerParams(
            dimension_semantics=("parallel",), vmem_limit_bytes=V7X_VMEM_LIMIT),
        name="attn_decode",
    )(layer, slopes, z3, *args)


def _ret_decode_kernel(l_ref, lg_ref, z_ref, st_ref, gng_ref, gnb_ref, o_ref, sn_ref):
    row = z_ref[0]
    eye = (lax.broadcasted_iota(jnp.int32, (DK_B, DK_B), 0)
           == lax.broadcasted_iota(jnp.int32, (DK_B, DK_B), 1))
    ones11 = jnp.ones((1, 1), F32)
    v0 = 2 * WIDTH_BQK
    g0 = v0 + WIDTH_BV
    for h in range(N_HEADS_B):
        qh = row[:, h * DK_B:(h + 1) * DK_B]
        kh = row[:, WIDTH_BQK + h * DK_B:WIDTH_BQK + (h + 1) * DK_B] * KB_SCALE
        vh = row[:, v0 + h * DV_B:v0 + (h + 1) * DV_B]
        gh = row[:, g0 + h * DV_B:g0 + (h + 1) * DV_B]
        cols = slice(h * DV_B, (h + 1) * DV_B)
        gamma = jnp.exp(ones11 * lg_ref[h])
        q_col = jnp.sum(jnp.where(eye, qh, 0.0), axis=1, keepdims=True)
        k_col = jnp.sum(jnp.where(eye, kh, 0.0), axis=1, keepdims=True)
        state = st_ref[h]
        qk = jnp.sum(qh * kh, axis=1, keepdims=True)
        o = qk * vh + jnp.sum(q_col * state, axis=0, keepdims=True) * gamma
        sn_ref[h] = state * gamma + k_col * vh
        res = _group_norm_gate(o, gh, gng_ref[:, cols], gnb_ref[:, cols])
        o_ref[0, :, cols] = res.astype(BF16)


def _ret_decode(layer, log_g, z3, state_ret, gn_g, gn_b):
    nb = state_ret.shape[1]
    width = 2 * WIDTH_BQK + 2 * WIDTH_BV
    return pl.pallas_call(
        _ret_decode_kernel,
        grid_spec=pltpu.PrefetchScalarGridSpec(
            num_scalar_prefetch=1,
            grid=(nb,),
            in_specs=[
                pl.BlockSpec(memory_space=pltpu.SMEM),
                pl.BlockSpec((1, 1, width), lambda b, l: (b, 0, 1)),
                pl.BlockSpec((None, None, N_HEADS_B, DK_B, DV_B), lambda b, l: (l[0], b, 0, 0, 0)),
                pl.BlockSpec((None, 1, WIDTH_BV), lambda b, l: (l[0], 0, 0)),
                pl.BlockSpec((None, 1, WIDTH_BV), lambda b, l: (l[0], 0, 0)),
            ],
            out_specs=[
                pl.BlockSpec((1, 1, WIDTH_BV), lambda b, l: (b, 0, 0)),
                pl.BlockSpec((None, N_HEADS_B, DK_B, DV_B), lambda b, l: (b, 0, 0, 0)),
            ],
        ),
        out_shape=[jax.ShapeDtypeStruct((nb, 1, WIDTH_BV), BF16),
                   jax.ShapeDtypeStruct((nb, N_HEADS_B, DK_B, DV_B), F32)],
        compiler_params=pltpu.CompilerParams(
            dimension_semantics=("parallel",), vmem_limit_bytes=V7X_VMEM_LIMIT),
        name="ret_decode",
    )(layer, log_g, z3, state_ret, gn_g, gn_b)


def kernel(x_prompt, x_sample, cache_k, cache_v, state_ret, p_prompt, p_sample, w_in, w_out, gn_g,
           gn_b, ffn1_w1, ffn1_w3, ffn1_w2, ffn2_w1, ffn2_w3, ffn2_w2, w_ple, w_gate, ln_g, ln_b):
    batch, seq, _ = x_prompt.shape
    dec_batch, dec_seq, _ = x_sample.shape
    assert dec_seq == 1 and seq == BAND * DILATIONS[-1][1]
    assert cache_k.shape[2] == BAND * DILATIONS[-1][1]
    m_p, m_s = batch * seq, dec_batch * dec_seq
    tm_p = 1024
    assert m_p % tm_p == 0

    f1 = (ffn1_w1, ffn1_w3, ffn1_w2)
    f2 = (ffn2_w1, ffn2_w3, ffn2_w2)
    w_in_b = w_in.astype(BF16)
    w_kv_b = w_in[:, :, WIDTH_A:3 * WIDTH_A].astype(BF16)
    ln_g4 = ln_g.reshape(DEPTH, 4, 1, D_MODEL)
    ln_b4 = ln_b.reshape(DEPTH, 4, 1, D_MODEL)
    gn_g3 = gn_g.reshape(DEPTH, 1, WIDTH_BV)
    gn_b3 = gn_b.reshape(DEPTH, 1, WIDTH_BV)
    p_p = p_prompt.reshape(DEPTH, m_p, PLE_DIM)
    p_s = p_sample.reshape(DEPTH, m_s, PLE_DIM)

    heads_a = jnp.arange(1, N_HEADS_A + 1, dtype=F32)
    slopes = jnp.exp2(-8.0 * heads_a / N_HEADS_A)
    log_g = jnp.log(1.0 - jnp.exp2(-5.0 - jnp.arange(N_HEADS_B, dtype=F32)))

    def layer_step(carry, i):
        xp, xs, k_p, v_p, f1_b = carry
        layer = i.reshape(1)
        nxt = jnp.minimum(i + 1, DEPTH - 1)

        xp, xs, *f2_b = _ffn_ln(jnp.stack([i, i]), xp, xs, *f1_b, *f2, ln_g4, ln_b4, 0, tm_p)

        zp = _in_proj(layer, xp, w_in_b, tm_p, WIDTH_A, 4, skip_after_first=2)
        k_p, v_p, kv_p = _in_proj_kv(layer, xp, w_kv_b, k_p, v_p, 512)
        zs = _in_proj(layer, xs, w_in_b, m_s, WIDTH_A, IN_WIDTH // WIDTH_A)

        oa_p = _attn_prompt(layer, zp, kv_p, slopes, batch, seq)
        ob_p, ret_p = _ret_prompt(layer, zp, log_g, gn_g3, gn_b3, batch, seq)
        zs3 = zs.reshape(m_s, 1, IN_WIDTH)
        oa_s = _attn_decode(layer, slopes, zs3, cache_k, cache_v).reshape(m_s, WIDTH_A)
        ob_s, ret_s = _ret_decode(layer, log_g, zs3, state_ret, gn_g3, gn_b3)
        ob_s = ob_s.reshape(m_s, WIDTH_BV)

        xp = _out_proj_ln(layer, oa_p, ob_p, xp, w_out, ln_g4, ln_b4, 512)
        xs = _out_proj_ln(layer, oa_s, ob_s, xs, w_out, ln_g4, ln_b4, m_s)

        xp, xs, *f1_b = _ffn_ln(jnp.stack([i, nxt]), xp, xs, *f2_b, *f1, ln_g4, ln_b4, 2, tm_p)

        xp = _ple_ln(layer, xp, p_p, w_ple, w_gate, ln_g4, ln_b4, 512)
        xs = _ple_ln(layer, xs, p_s, w_ple, w_gate, ln_g4, ln_b4, m_s)

        k_s = zs[:, WIDTH_A:2 * WIDTH_A].reshape(dec_batch, dec_seq, N_HEADS_A, HEAD_DIM)
        v_s = zs[:, 2 * WIDTH_A:3 * WIDTH_A].reshape(dec_batch, dec_seq, N_HEADS_A, HEAD_DIM)
        return (xp, xs, k_p, v_p, tuple(f1_b)), (ret_p, k_s, v_s, ret_s)

    kv_shape = (DEPTH, m_p, N_HEADS_A, HEAD_DIM)
    (xp, xs, k_p, v_p, _), (ret_p, k_s, v_s, ret_s) = lax.scan(
        layer_step,
        (x_prompt.reshape(m_p, D_MODEL), x_sample.reshape(m_s, D_MODEL),
         jnp.zeros(kv_shape, F32), jnp.zeros(kv_shape, F32),
         tuple(w[0].astype(BF16) for w in f1)),
        jnp.arange(DEPTH, dtype=jnp.int32))
    kv_out = (DEPTH, batch, seq, N_HEADS_A, HEAD_DIM)
    return (xp.reshape(batch, seq, D_MODEL), xs.reshape(dec_batch, dec_seq, D_MODEL),
            k_p.reshape(kv_out), v_p.reshape(kv_out), ret_p, k_s, v_s, ret_s)
```

```python
import functools

import jax
import jax.numpy as jnp
from jax import lax
from jax.experimental import pallas as pl
from jax.experimental.pallas import tpu as pltpu

F32 = jnp.float32
BF16 = jnp.bfloat16

D_MODEL = 2048
DEPTH = 4
HEAD_DIM = 128
WIDTH_A = 1024
N_HEADS_A = 8
DILATIONS = ((128, 1), (512, 4), (2048, 16))
N_HEADS_B = 8
DV_B = 128
DK_B = 64
WIDTH_BQK = N_HEADS_B * DK_B
WIDTH_BV = N_HEADS_B * DV_B
IN_WIDTH = 3 * WIDTH_A + 2 * WIDTH_BQK + 2 * WIDTH_BV
FFN_DIM = 5632
PLE_DIM = 256
RET_CHUNK = 128
LN_EPS = 1e-5
GN_EPS = 1e-5
NEG_INF = -1e30
DEEPNORM_ALPHA = (2 * DEPTH) ** 0.25
ATTN_SCALE = HEAD_DIM ** -0.5
KB_SCALE = DK_B ** -0.5

BAND = 128
V7X_VMEM_LIMIT = 56 * 1024 * 1024
V7X_VMEM_LIMIT_FFN = 62 * 1024 * 1024
FFN_TILE_F = 512
LN_ROWS = 128
ATTN_UNROLL = 16
ATTN_GROUP = 4
RET_BATCH = 2
LOG2E = 1.4426950408889634
LN2 = 0.6931471805599453


def _dot(a, b):
    return jnp.dot(a, b, preferred_element_type=F32)


def _dot_nt(a, b):
    return lax.dot_general(a, b, (((1,), (1,)), ((), ())), preferred_element_type=F32)


def _layer_norm_rows(y, g, b):
    mu = jnp.mean(y, axis=-1, keepdims=True)
    d = y - mu
    var = jnp.mean(d * d, axis=-1, keepdims=True)
    return d * lax.rsqrt(var + LN_EPS) * g + b


def _residual_ln_epilogue(x_ref, o_ref, g_ref, b_ref, branch_scale):
    rows = o_ref.shape[0]
    step = min(rows, LN_ROWS)
    g = g_ref[...]
    b = b_ref[...]

    def body(c, carry):
        r = pl.multiple_of(c * step, step)
        y = DEEPNORM_ALPHA * x_ref[pl.ds(r, step), :] + branch_scale * o_ref[pl.ds(r, step), :]
        o_ref[pl.ds(r, step), :] = _layer_norm_rows(y, g, b)
        return carry

    lax.fori_loop(0, rows // step, body, 0)


def _swiglu_step(xb, w1b, w3b, w2b):
    h1 = _dot(xb, w1b)
    h3 = _dot(xb, w3b)
    act = (h1 * jax.nn.sigmoid(h1) * h3).astype(BF16)
    return _dot(act, w2b)


def _ffn_kernel(l_ref, x_ref, xs_ref, w1_ref, w3_ref, w2_ref, c1_ref, c3_ref, c2_ref, g_ref, b_ref,
                o_ref, os_ref, n1_ref, n3_ref, n2_ref, xb_ref, xsb_ref, *, nf):
    i = pl.program_id(0)
    f = pl.program_id(1)

    @pl.when(f == 0)
    def _():
        xb_ref[...] = x_ref[...].astype(BF16)
        o_ref[...] = jnp.zeros_like(o_ref)

    @pl.when((f == 0) & (i == 0))
    def _():
        xsb_ref[...] = xs_ref[...].astype(BF16)
        os_ref[...] = jnp.zeros_like(os_ref)

    n1_ref[...] = c1_ref[...].astype(BF16)
    n3_ref[...] = c3_ref[...].astype(BF16)
    n2_ref[...] = c2_ref[...].astype(BF16)

    o_ref[...] += _swiglu_step(xb_ref[...], w1_ref[...], w3_ref[...], w2_ref[...])

    @pl.when(i == 0)
    def _():
        os_ref[...] += _swiglu_step(xsb_ref[...], w1_ref[...], w3_ref[...], w2_ref[...])

    @pl.when(f == nf - 1)
    def _():
        _residual_ln_epilogue(x_ref, o_ref, g_ref, b_ref, 0.5)

    @pl.when((f == nf - 1) & (i == 0))
    def _():
        _residual_ln_epilogue(xs_ref, os_ref, g_ref, b_ref, 0.5)


def _ffn_ln(layers, x, xs, w1, w3, w2, c1, c3, c2, ln_g, ln_b, ln_idx, tm):
    m, ms = x.shape[0], xs.shape[0]
    nf = FFN_DIM // FFN_TILE_F
    cr = D_MODEL // (m // tm)
    return pl.pallas_call(
        functools.partial(_ffn_kernel, nf=nf),
        grid_spec=pltpu.PrefetchScalarGridSpec(
            num_scalar_prefetch=1,
            grid=(m // tm, nf),
            in_specs=[
                pl.BlockSpec((tm, D_MODEL), lambda i, f, l: (i, 0)),
                pl.BlockSpec((ms, D_MODEL), lambda i, f, l: (0, 0)),
                pl.BlockSpec((D_MODEL, FFN_TILE_F), lambda i, f, l: (0, f)),
                pl.BlockSpec((D_MODEL, FFN_TILE_F), lambda i, f, l: (0, f)),
                pl.BlockSpec((FFN_TILE_F, D_MODEL), lambda i, f, l: (f, 0)),
                pl.BlockSpec((None, cr, FFN_TILE_F), lambda i, f, l: (l[1], i, f)),
                pl.BlockSpec((None, cr, FFN_TILE_F), lambda i, f, l: (l[1], i, f)),
                pl.BlockSpec((None, FFN_TILE_F, cr), lambda i, f, l: (l[1], f, i)),
                pl.BlockSpec((None, None, 1, D_MODEL), lambda i, f, l: (l[0], ln_idx, 0, 0)),
                pl.BlockSpec((None, None, 1, D_MODEL), lambda i, f, l: (l[0], ln_idx, 0, 0)),
            ],
            out_specs=[
                pl.BlockSpec((tm, D_MODEL), lambda i, f, l: (i, 0)),
                pl.BlockSpec((ms, D_MODEL), lambda i, f, l: (0, 0)),
                pl.BlockSpec((cr, FFN_TILE_F), lambda i, f, l: (i, f)),
                pl.BlockSpec((cr, FFN_TILE_F), lambda i, f, l: (i, f)),
                pl.BlockSpec((FFN_TILE_F, cr), lambda i, f, l: (f, i)),
            ],
            scratch_shapes=[pltpu.VMEM((tm, D_MODEL), BF16), pltpu.VMEM((ms, D_MODEL), BF16)],
        ),
        out_shape=[jax.ShapeDtypeStruct((m, D_MODEL), F32),
                   jax.ShapeDtypeStruct((ms, D_MODEL), F32),
                   jax.ShapeDtypeStruct((D_MODEL, FFN_DIM), BF16),
                   jax.ShapeDtypeStruct((D_MODEL, FFN_DIM), BF16),
                   jax.ShapeDtypeStruct((FFN_DIM, D_MODEL), BF16)],
        compiler_params=pltpu.CompilerParams(
            dimension_semantics=("arbitrary", "arbitrary"), vmem_limit_bytes=V7X_VMEM_LIMIT_FFN),
        name="ffn_ln",
    )(layers, x, xs, w1, w3, w2, c1, c3, c2, ln_g, ln_b)


def _proj_kernel(l_ref, x_ref, w_ref, o_ref, xb_ref):
    @pl.when(pl.program_id(1) == 0)
    def _():
        xb_ref[...] = x_ref[...].astype(BF16)

    o_ref[...] = _dot(xb_ref[...], w_ref[...])


def _in_proj(layer, x, w_in, tm, tn, n_blocks, skip_after_first=0):
    m = x.shape[0]
    col = lambda n: n + jnp.where(n > 0, skip_after_first, 0)
    return pl.pallas_call(
        _proj_kernel,
        grid_spec=pltpu.PrefetchScalarGridSpec(
            num_scalar_prefetch=1,
            grid=(m // tm, n_blocks),
            in_specs=[
                pl.BlockSpec((tm, D_MODEL), lambda i, n, l: (i, 0)),
                pl.BlockSpec((None, D_MODEL, tn), lambda i, n, l: (l[0], 0, col(n))),
            ],
            out_specs=pl.BlockSpec((tm, tn), lambda i, n, l: (i, n)),
            scratch_shapes=[pltpu.VMEM((tm, D_MODEL), BF16)],
        ),
        out_shape=jax.ShapeDtypeStruct((m, n_blocks * tn), F32),
        compiler_params=pltpu.CompilerParams(
            dimension_semantics=("parallel", "arbitrary"), vmem_limit_bytes=V7X_VMEM_LIMIT),
        name="in_proj",
    )(layer, x, w_in)


def _proj_kv_kernel(l_ref, x_ref, w_ref, k_in_ref, v_in_ref, k_ref, v_ref, kv_ref):
    xb = x_ref[...].astype(BF16)
    for n, dst_ref in enumerate((k_ref, v_ref)):
        kv_ref[:, n * WIDTH_A:(n + 1) * WIDTH_A] = _dot(xb, w_ref[:, n * WIDTH_A:(n + 1) * WIDTH_A])
        for h in range(N_HEADS_A):
            c0 = n * WIDTH_A + h * HEAD_DIM
            dst_ref[:, h, :] = kv_ref[:, c0:c0 + HEAD_DIM]


def _in_proj_kv(layer, x, w_kv, k_all, v_all, tm):
    m = x.shape[0]
    out_spec = pl.BlockSpec((None, tm, N_HEADS_A, HEAD_DIM), lambda i, l: (l[0], i, 0, 0))
    return pl.pallas_call(
        _proj_kv_kernel,
        grid_spec=pltpu.PrefetchScalarGridSpec(
            num_scalar_prefetch=1,
            grid=(m // tm,),
            in_specs=[
                pl.BlockSpec((tm, D_MODEL), lambda i, l: (i, 0)),
                pl.BlockSpec((None, D_MODEL, 2 * WIDTH_A), lambda i, l: (l[0], 0, 0)),
                pl.BlockSpec(memory_space=pl.ANY),
                pl.BlockSpec(memory_space=pl.ANY),
            ],
            out_specs=[out_spec, out_spec, pl.BlockSpec((tm, 2 * WIDTH_A), lambda i, l: (i, 0))],
        ),
        out_shape=[jax.ShapeDtypeStruct(k_all.shape, F32), jax.ShapeDtypeStruct(v_all.shape, F32),
                   jax.ShapeDtypeStruct((m, 2 * WIDTH_A), F32)],
        input_output_aliases={3: 0, 4: 1},
        compiler_params=pltpu.CompilerParams(
            dimension_semantics=("parallel",), vmem_limit_bytes=V7X_VMEM_LIMIT),
        name="in_proj_kv",
    )(layer, x, w_kv, k_all, v_all)


def _out_proj_kernel(l_ref, oa_ref, ob_ref, x_ref, w_ref, g_ref, b_ref, o_ref, wb_ref):
    @pl.when(pl.program_id(0) == 0)
    def _():
        wb_ref[...] = w_ref[...].astype(BF16)

    o_ref[...] = _dot(oa_ref[...], wb_ref[0:WIDTH_A, :]) + _dot(ob_ref[...], wb_ref[WIDTH_A:, :])
    _residual_ln_epilogue(x_ref, o_ref, g_ref, b_ref, 1.0)


def _out_proj_ln(layer, oa, ob, x, w_out, ln_g, ln_b, tm):
    m = x.shape[0]
    return pl.pallas_call(
        _out_proj_kernel,
        grid_spec=pltpu.PrefetchScalarGridSpec(
            num_scalar_prefetch=1,
            grid=(m // tm,),
            in_specs=[
                pl.BlockSpec((tm, WIDTH_A), lambda i, l: (i, 0)),
                pl.BlockSpec((tm, WIDTH_BV), lambda i, l: (i, 0)),
                pl.BlockSpec((tm, D_MODEL), lambda i, l: (i, 0)),
                pl.BlockSpec((None, WIDTH_A + WIDTH_BV, D_MODEL), lambda i, l: (l[0], 0, 0),
                             pipeline_mode=pl.Buffered(1)),
                pl.BlockSpec((None, None, 1, D_MODEL), lambda i, l: (l[0], 1, 0, 0)),
                pl.BlockSpec((None, None, 1, D_MODEL), lambda i, l: (l[0], 1, 0, 0)),
            ],
            out_specs=pl.BlockSpec((tm, D_MODEL), lambda i, l: (i, 0)),
            scratch_shapes=[pltpu.VMEM((WIDTH_A + WIDTH_BV, D_MODEL), BF16)],
        ),
        out_shape=jax.ShapeDtypeStruct((m, D_MODEL), F32),
        compiler_params=pltpu.CompilerParams(
            dimension_semantics=("arbitrary",), vmem_limit_bytes=V7X_VMEM_LIMIT),
        name="out_proj_ln",
    )(layer, oa, ob, x, w_out, ln_g, ln_b)


def _ple_kernel(l_ref, x_ref, p_ref, wp_ref, wg_ref, g_ref, b_ref, o_ref, wpb_ref, wgb_ref):
    @pl.when(pl.program_id(0) == 0)
    def _():
        wpb_ref[...] = wp_ref[...].astype(BF16)
        wgb_ref[...] = wg_ref[...].astype(BF16)

    gate = jax.nn.sigmoid(_dot(x_ref[...].astype(BF16), wgb_ref[...]))
    o_ref[...] = _dot(p_ref[...].astype(BF16), wpb_ref[...]) * gate
    _residual_ln_epilogue(x_ref, o_ref, g_ref, b_ref, 1.0)


def _ple_ln(layer, x, p, w_ple, w_gate, ln_g, ln_b, tm):
    m = x.shape[0]
    return pl.pallas_call(
        _ple_kernel,
        grid_spec=pltpu.PrefetchScalarGridSpec(
            num_scalar_prefetch=1,
            grid=(m // tm,),
            in_specs=[
                pl.BlockSpec((tm, D_MODEL), lambda i, l: (i, 0)),
                pl.BlockSpec((None, tm, PLE_DIM), lambda i, l: (l[0], i, 0)),
                pl.BlockSpec((None, PLE_DIM, D_MODEL), lambda i, l: (l[0], 0, 0),
                             pipeline_mode=pl.Buffered(1)),
                pl.BlockSpec((None, D_MODEL, D_MODEL), lambda i, l: (l[0], 0, 0),
                             pipeline_mode=pl.Buffered(1)),
                pl.BlockSpec((None, None, 1, D_MODEL), lambda i, l: (l[0], 3, 0, 0)),
                pl.BlockSpec((None, None, 1, D_MODEL), lambda i, l: (l[0], 3, 0, 0)),
            ],
            out_specs=pl.BlockSpec((tm, D_MODEL), lambda i, l: (i, 0)),
            scratch_shapes=[pltpu.VMEM((PLE_DIM, D_MODEL), BF16),
                            pltpu.VMEM((D_MODEL, D_MODEL), BF16)],
        ),
        out_shape=jax.ShapeDtypeStruct((m, D_MODEL), F32),
        compiler_params=pltpu.CompilerParams(
            dimension_semantics=("arbitrary",), vmem_limit_bytes=V7X_VMEM_LIMIT),
        name="ple_ln",
    )(layer, x, p, w_ple, w_gate, ln_g, ln_b)


def _merge_pair(o_a, l_a, o_b, l_b):
    m = jnp.maximum(l_a, l_b)
    e_a = jnp.exp(l_a - m)
    e_b = jnp.exp(l_b - m)
    den = e_a + e_b
    return (e_a * o_a + e_b * o_b) / den, m + jnp.log(den)


def _attn_prompt_kernel(l_ref, slope_ref, q_ref, k_ref, v_ref, o_ref,
                        x4_ref, x16_ref, va_ref, s_ref, p_ref, ls_ref, bias_ref, og_ref, lse_ref,
                        *, seq):
    nblk = seq // BAND
    slab = ATTN_GROUP * BAND
    part4 = seq // 4
    slope = slope_ref[pl.program_id(1)]

    nat = (q_ref, k_ref, v_ref)
    for t in range(3):
        for a in range(4):
            for c in range(part4 // BAND):
                x4_ref[t, pl.ds(a * part4 + c * BAND, BAND), :] = (
                    nat[t][pl.ds(a + 4 * c * BAND, BAND, stride=4), :])
    for t in range(3):
        for a in range(4):
            for b in range(4):
                x16_ref[t, pl.ds((a + 4 * b) * BAND, BAND), :] = (
                    x4_ref[t, pl.ds(a * part4 + b, BAND, stride=4), :])

    ones = jnp.ones((slab, HEAD_DIM), BF16)
    for slot, src in enumerate((lambda r: v_ref[r, :], lambda r: x4_ref[2, r, :],
                                lambda r: x16_ref[2, r, :])):
        for c in range(seq // slab):
            r = pl.ds(c * slab, slab)
            va_ref[slot, r, 0:HEAD_DIM] = src(r).astype(BF16)
            va_ref[slot, r, HEAD_DIM:2 * HEAD_DIM] = ones

    row = lax.broadcasted_iota(jnp.int32, (slab, 2 * BAND), 0)
    col = lax.broadcasted_iota(jnp.int32, (slab, 2 * BAND), 1)
    dist = jnp.bitwise_and(row, BAND - 1) + BAND - col
    in_band = (dist >= 0) & (dist <= BAND)
    starts_class = (row < BAND) & (col < BAND)
    distf = dist.astype(F32)
    tile1 = jnp.where(in_band, -(slope * (1 * LOG2E)) * distf, NEG_INF)
    bias_ref[1] = tile1
    bias_ref[0] = jnp.where(starts_class, NEG_INF, tile1)
    bias_ref[2] = jnp.where(starts_class | ~in_band, NEG_INF, -(slope * (4 * LOG2E)) * distf)
    dist_own = dist[:, BAND:]
    bias_ref[3, :, 0:BAND] = jnp.where(dist_own >= 0,
                                       -(slope * (16 * LOG2E)) * dist_own.astype(F32), NEG_INF)

    def band_pass(qf, kf, vf, two, first_tile, rest_tile, store):
        kw = 2 * BAND if two else BAND

        if two:
            s_ref[0:BAND, 0:BAND] = jnp.zeros((BAND, BAND), F32)
            s_ref[0:BAND, BAND:kw] = _dot_nt(qf(0, BAND).astype(BF16), kf(0, BAND).astype(BF16))

            def scores(j, carry):
                r = pl.multiple_of(j * BAND, BAND)
                rp = pl.multiple_of(r - BAND, BAND)
                s_ref[pl.ds(r, BAND), :] = _dot_nt(qf(r, BAND).astype(BF16),
                                                   kf(rp, kw).astype(BF16))
                return carry

            lax.fori_loop(1, nblk, scores, 0, unroll=ATTN_UNROLL)
        else:
            def scores(j, carry):
                r = pl.multiple_of(j * BAND, BAND)
                s_ref[pl.ds(r, BAND), 0:kw] = _dot_nt(qf(r, BAND).astype(BF16),
                                                      kf(r, BAND).astype(BF16))
                return carry

            lax.fori_loop(0, nblk, scores, 0, unroll=ATTN_UNROLL)

        def softmax_slab(r, tile):
            s = s_ref[pl.ds(r, slab), 0:kw] * (ATTN_SCALE * LOG2E) + bias_ref[tile, :, 0:kw]
            m = jnp.max(s, axis=-1, keepdims=True)
            p = jnp.exp2(s - m)
            p_ref[pl.ds(r, slab), 0:kw] = p.astype(BF16)
            ls_ref[pl.ds(r, slab), :] = jnp.broadcast_to(m * LN2, (slab, HEAD_DIM))

        first_rest = 0
        if first_tile != rest_tile:
            softmax_slab(0, first_tile)
            first_rest = 1

        def rest_slab(g, carry):
            softmax_slab(pl.multiple_of(g * slab, slab), rest_tile)
            return carry

        lax.fori_loop(first_rest, nblk // ATTN_GROUP, rest_slab, 0, unroll=True)

        def finish(j, r, o_l):
            l = o_l[:, HEAD_DIM:]
            store(j, o_l[:, 0:HEAD_DIM] / l, ls_ref[pl.ds(r, BAND), :] + jnp.log(l))

        if two:
            finish(0, 0, _dot(p_ref[0:BAND, BAND:kw], vf(0, BAND)))

            def values(j, carry):
                r = pl.multiple_of(j * BAND, BAND)
                rp = pl.multiple_of(r - BAND, BAND)
                finish(j, r, _dot(p_ref[pl.ds(r, BAND), :], vf(rp, kw)))
                return carry

            lax.fori_loop(1, nblk, values, 0, unroll=ATTN_UNROLL)
        else:
            def values(j, carry):
                r = pl.multiple_of(j * BAND, BAND)
                finish(j, r, _dot(p_ref[pl.ds(r, BAND), 0:kw], vf(r, BAND)))
                return carry

            lax.fori_loop(0, nblk, values, 0, unroll=ATTN_UNROLL)

    def store_contiguous(slot):
        def store(j, o, lse):
            og_ref[slot, pl.ds(j * BAND, BAND), :] = o
            lse_ref[slot, pl.ds(j * BAND, BAND), :] = lse
        return store

    def store_16_as_4(j, o, lse):
        start = jnp.bitwise_and(j, 3) * part4 + jnp.right_shift(j, 2)
        og_ref[2, pl.ds(start, BAND, stride=4), :] = o
        lse_ref[2, pl.ds(start, BAND, stride=4), :] = lse

    values_of = lambda slot: (lambda start, size: va_ref[slot, pl.ds(start, size), :])
    natural = [lambda start, size, ref=ref: ref[pl.ds(start, size), :] for ref in nat[:2]]
    by4 = [lambda start, size, t=t: x4_ref[t, pl.ds(start, size), :] for t in range(2)]
    by16 = [lambda start, size, t=t: x16_ref[t, pl.ds(start, size), :] for t in range(2)]
    natural.append(values_of(0))
    by4.append(values_of(1))
    by16.append(values_of(2))
    band_pass(*natural, True, 0, 1, store_contiguous(0))
    band_pass(*by4, True, 2, 2, store_contiguous(1))
    band_pass(*by16, False, 3, 3, store_16_as_4)

    for a in range(4):
        for c in range(part4 // BAND):
            src = pl.ds(a * part4 + c * BAND, BAND)
            dst = pl.ds(a + 4 * c * BAND, BAND, stride=4)
            o, lse = _merge_pair(og_ref[1, src, :], lse_ref[1, src, :],
                                 og_ref[2, src, :], lse_ref[2, src, :])
            og_ref[3, dst, :] = o
            lse_ref[3, dst, :] = lse

    def final_merge(c, carry):
        rows = pl.ds(pl.multiple_of(c * BAND, BAND), BAND)
        o, _ = _merge_pair(og_ref[0, rows, :], lse_ref[0, rows, :],
                           og_ref[3, rows, :], lse_ref[3, rows, :])
        o_ref[rows, :] = o.astype(BF16)
        return carry

    lax.fori_loop(0, nblk, final_merge, 0, unroll=2)


def _attn_prompt(layer, zqb, kv, slopes, batch, seq):
    slab = ATTN_GROUP * BAND
    return pl.pallas_call(
        functools.partial(_attn_prompt_kernel, seq=seq),
        grid_spec=pltpu.PrefetchScalarGridSpec(
            num_scalar_prefetch=1,
            grid=(batch, N_HEADS_A),
            in_specs=[
                pl.BlockSpec(memory_space=pltpu.SMEM),
                pl.BlockSpec((seq, HEAD_DIM), lambda b, h, l: (b, h)),
                pl.BlockSpec((seq, HEAD_DIM), lambda b, h, l: (b, h)),
                pl.BlockSpec((seq, HEAD_DIM), lambda b, h, l: (b, N_HEADS_A + h)),
            ],
            out_specs=pl.BlockSpec((seq, HEAD_DIM), lambda b, h, l: (b, h)),
            scratch_shapes=[
                pltpu.VMEM((3, seq, HEAD_DIM), F32),
                pltpu.VMEM((3, seq, HEAD_DIM), F32),
                pltpu.VMEM((3, seq, 2 * HEAD_DIM), BF16),
                pltpu.VMEM((seq, 2 * BAND), F32),
                pltpu.VMEM((seq, 2 * BAND), BF16),
                pltpu.VMEM((seq, HEAD_DIM), F32),
                pltpu.VMEM((4, slab, 2 * BAND), F32),
                pltpu.VMEM((4, seq, HEAD_DIM), F32),
                pltpu.VMEM((4, seq, HEAD_DIM), F32),
            ],
        ),
        out_shape=jax.ShapeDtypeStruct((batch * seq, WIDTH_A), BF16),
        compiler_params=pltpu.CompilerParams(
            dimension_semantics=("parallel", "parallel"), vmem_limit_bytes=V7X_VMEM_LIMIT),
        name="attn_prompt",
    )(layer, slopes, zqb, kv, kv)


def _group_norm_gate(o, gate, gn_g, gn_b):
    mu = jnp.mean(o, axis=-1, keepdims=True)
    d = o - mu
    var = jnp.mean(d * d, axis=-1, keepdims=True)
    on = d * lax.rsqrt(var + GN_EPS)
    return (on * gn_g + gn_b) * (gate * jax.nn.sigmoid(gate))


def _ret_prompt_kernel(l_ref, lg_ref, q_ref, k_ref, v_ref, gate_ref, gng_ref, gnb_ref, o_ref,
                       s_ref, st_ref, *, seq):
    pair = pl.program_id(1)
    c_len = RET_CHUNK
    lane = lax.broadcasted_iota(jnp.int32, (c_len, 2 * DK_B), 1)
    pos_r = lax.broadcasted_iota(jnp.int32, (c_len, c_len), 0)
    pos_c = lax.broadcasted_iota(jnp.int32, (c_len, c_len), 1)
    rel = (pos_r - pos_c).astype(F32)
    pos = lax.broadcasted_iota(jnp.int32, (c_len, 1), 0).astype(F32)
    ones11 = jnp.ones((1, 1), F32)

    heads = []
    for hx in range(2):
        lg = lg_ref[2 * pair + hx]
        decay_mask = jnp.where(rel >= 0, jnp.exp(jnp.maximum(rel, 0.0) * lg), 0.0)
        q_decay = jnp.exp((pos + 1.0) * lg)
        k_decay = jnp.exp((c_len - 1.0 - pos) * lg)
        chunk_decay = jnp.exp((c_len * ones11) * lg)
        heads.append((decay_mask, q_decay, k_decay, chunk_decay))

    st_ref[...] = jnp.zeros_like(st_ref)

    def chunk(c, carry):
        units = []
        for bi in range(RET_BATCH):
            r = pl.multiple_of(bi * seq + c * c_len, c_len)
            q2 = q_ref[pl.ds(r, c_len), :]
            k2 = k_ref[pl.ds(r, c_len), :] * KB_SCALE
            kb = k2.astype(BF16)
            for hx in range(2):
                qx = jnp.where((lane >= DK_B) if hx else (lane < DK_B), q2, 0.0).astype(BF16)
                units.append((2 * bi + hx, hx, r, qx, kb, k2))
        inners = [_dot_nt(qx, kb) * heads[hx][0] for _, hx, _, qx, kb, _ in units]
        vxs = [v_ref[pl.ds(r, c_len), hx * DV_B:(hx + 1) * DV_B].astype(BF16)
               for _, hx, r, _, _, _ in units]
        states = [st_ref[u] for u, *_ in units]
        intras = [_dot(inner.astype(BF16), vx) for inner, vx in zip(inners, vxs)]
        crosses = [_dot(qx, state.astype(BF16)) for (_, _, _, qx, _, _), state in zip(units, states)]
        updates = [_dot((k2 * heads[hx][2]).T.astype(BF16), vx)
                   for (_, hx, _, _, _, k2), vx in zip(units, vxs)]
        for (u, hx, r, _, _, _), intra, cross, update, state in zip(units, intras, crosses, updates,
                                                                  states):
            _, q_decay, _, chunk_decay = heads[hx]
            cols = slice(hx * DV_B, (hx + 1) * DV_B)
            st_ref[u] = state * chunk_decay + update
            res = _group_norm_gate(intra + cross * q_decay, gate_ref[pl.ds(r, c_len), cols],
                                   gng_ref[:, cols], gnb_ref[:, cols])
            o_ref[pl.ds(r, c_len), cols] = res.astype(BF16)
        return carry

    lax.fori_loop(0, seq // c_len, chunk, 0, unroll=4)
    for bi in range(RET_BATCH):
        s_ref[bi, 0] = st_ref[2 * bi, 0:DK_B, :]
        s_ref[bi, 1] = st_ref[2 * bi + 1, DK_B:2 * DK_B, :]


def _ret_prompt(layer, z, log_g, gn_g, gn_b, batch, seq):
    assert batch % RET_BATCH == 0
    rows = RET_BATCH * seq
    qcol = WIDTH_A // (2 * DK_B)
    kcol = qcol + WIDTH_BQK // (2 * DK_B)
    vcol = (WIDTH_A + 2 * WIDTH_BQK) // (2 * DV_B)
    gcol = vcol + WIDTH_BV // (2 * DV_B)
    return pl.pallas_call(
        functools.partial(_ret_prompt_kernel, seq=seq),
        grid_spec=pltpu.PrefetchScalarGridSpec(
            num_scalar_prefetch=1,
            grid=(batch // RET_BATCH, N_HEADS_B // 2),
            in_specs=[
                pl.BlockSpec(memory_space=pltpu.SMEM),
                pl.BlockSpec((rows, 2 * DK_B), lambda b, p, l: (b, qcol + p)),
                pl.BlockSpec((rows, 2 * DK_B), lambda b, p, l: (b, kcol + p)),
                pl.BlockSpec((rows, 2 * DV_B), lambda b, p, l: (b, vcol + p)),
                pl.BlockSpec((rows, 2 * DV_B), lambda b, p, l: (b, gcol + p)),
                pl.BlockSpec((None, 1, 2 * DV_B), lambda b, p, l: (l[0], 0, p)),
                pl.BlockSpec((None, 1, 2 * DV_B), lambda b, p, l: (l[0], 0, p)),
            ],
            out_specs=[
                pl.BlockSpec((rows, 2 * DV_B), lambda b, p, l: (b, p)),
                pl.BlockSpec((RET_BATCH, 2, DK_B, DV_B), lambda b, p, l: (b, p, 0, 0)),
            ],
            scratch_shapes=[pltpu.VMEM((2 * RET_BATCH, 2 * DK_B, DV_B), F32)],
        ),
        out_shape=[jax.ShapeDtypeStruct((batch * seq, WIDTH_BV), BF16),
                   jax.ShapeDtypeStruct((batch, N_HEADS_B, DK_B, DV_B), F32)],
        compiler_params=pltpu.CompilerParams(
            dimension_semantics=("parallel", "parallel"), vmem_limit_bytes=V7X_VMEM_LIMIT),
        name="ret_prompt",
    )(layer, log_g, z, z, z, z, gn_g, gn_b)


def _attn_decode_kernel(l_ref, slope_ref, z_ref, k1_ref, k4_ref, k16_ref, v1_ref, v4_ref, v16_ref,
                        o_ref):
    row = z_ref[0]
    steps = (BAND - lax.broadcasted_iota(jnp.int32, (BAND, 1), 0)).astype(F32)
    views = ((k1_ref, v1_ref, 1), (k4_ref, v4_ref, 4), (k16_ref, v16_ref, 16))
    for h in range(N_HEADS_A):
        cols = slice(h * HEAD_DIM, (h + 1) * HEAD_DIM)
        qh = row[:, cols]
        k_new = row[:, WIDTH_A + h * HEAD_DIM:WIDTH_A + (h + 1) * HEAD_DIM]
        v_new = row[:, 2 * WIDTH_A + h * HEAD_DIM:2 * WIDTH_A + (h + 1) * HEAD_DIM]
        slope = slope_ref[h]
        s_new = jnp.sum(qh * k_new, axis=1, keepdims=True) * ATTN_SCALE
        outs, lses = [], []
        for k_ref, v_ref, dil in views:
            s = (jnp.sum(k_ref[:, h, :] * qh, axis=1, keepdims=True) * ATTN_SCALE
                 - slope * (steps * dil))
            m = jnp.maximum(jnp.max(s, axis=0, keepdims=True), s_new)
            e = jnp.exp(s - m)
            e_new = jnp.exp(s_new - m)
            l = jnp.sum(e, axis=0, keepdims=True) + e_new
            o = (jnp.sum(e * v_ref[:, h, :], axis=0, keepdims=True) + e_new * v_new) / l
            outs.append(o)
            lses.append(m + jnp.log(l))
        m = jnp.maximum(jnp.maximum(lses[0], lses[1]), lses[2])
        w = [jnp.exp(x - m) for x in lses]
        o = (w[0] * outs[0] + w[1] * outs[1] + w[2] * outs[2]) / (w[0] + w[1] + w[2])
        o_ref[0, :, cols] = o.astype(BF16)


def _attn_decode(layer, slopes, z3, cache_k, cache_v):
    depth, nb, n_past = cache_k.shape[0], cache_k.shape[1], cache_k.shape[2]
    specs, args = [], []
    for cache in (cache_k, cache_v):
        for _, dil in DILATIONS:
            view = cache.reshape(depth, nb, n_past // dil, dil, N_HEADS_A, HEAD_DIM)
            last = n_past // dil // BAND - 1
            specs.append(pl.BlockSpec((None, None, BAND, None, N_HEADS_A, HEAD_DIM),
                                      lambda b, l, last=last: (l[0], b, last, 0, 0, 0)))
            args.append(view)
    return pl.pallas_call(
        _attn_decode_kernel,
        grid_spec=pltpu.PrefetchScalarGridSpec(
            num_scalar_prefetch=1,
            grid=(nb,),
            in_specs=[
                pl.BlockSpec(memory_space=pltpu.SMEM),
                pl.BlockSpec((1, 1, 3 * WIDTH_A), lambda b, l: (b, 0, 0)),
            ] + specs,
            out_specs=pl.BlockSpec((1, 1, WIDTH_A), lambda b, l: (b, 0, 0)),
        ),
        out_shape=jax.ShapeDtypeStruct((nb, 1, WIDTH_A), BF16),
        compiler_params=pltpu.CompilerParams(
            dimension_semantics=("parallel",), vmem_limit_bytes=V7X_VMEM_LIMIT),
        name="attn_decode",
    )(layer, slopes, z3, *args)


def _ret_decode_kernel(l_ref, lg_ref, z_ref, st_ref, gng_ref, gnb_ref, o_ref, sn_ref):
    row = z_ref[0]
    eye = (lax.broadcasted_iota(jnp.int32, (DK_B, DK_B), 0)
           == lax.broadcasted_iota(jnp.int32, (DK_B, DK_B), 1))
    ones11 = jnp.ones((1, 1), F32)
    v0 = 2 * WIDTH_BQK
    g0 = v0 + WIDTH_BV
    for h in range(N_HEADS_B):
        qh = row[:, h * DK_B:(h + 1) * DK_B]
        kh = row[:, WIDTH_BQK + h * DK_B:WIDTH_BQK + (h + 1) * DK_B] * KB_SCALE
        vh = row[:, v0 + h * DV_B:v0 + (h + 1) * DV_B]
        gh = row[:, g0 + h * DV_B:g0 + (h + 1) * DV_B]
        cols = slice(h * DV_B, (h + 1) * DV_B)
        gamma = jnp.exp(ones11 * lg_ref[h])
        q_col = jnp.sum(jnp.where(eye, qh, 0.0), axis=1, keepdims=True)
        k_col = jnp.sum(jnp.where(eye, kh, 0.0), axis=1, keepdims=True)
        state = st_ref[h]
        qk = jnp.sum(qh * kh, axis=1, keepdims=True)
        o = qk * vh + jnp.sum(q_col * state, axis=0, keepdims=True) * gamma
        sn_ref[h] = state * gamma + k_col * vh
        res = _group_norm_gate(o, gh, gng_ref[:, cols], gnb_ref[:, cols])
        o_ref[0, :, cols] = res.astype(BF16)


def _ret_decode(layer, log_g, z3, state_ret, gn_g, gn_b):
    nb = state_ret.shape[1]
    width = 2 * WIDTH_BQK + 2 * WIDTH_BV
    return pl.pallas_call(
        _ret_decode_kernel,
        grid_spec=pltpu.PrefetchScalarGridSpec(
            num_scalar_prefetch=1,
            grid=(nb,),
            in_specs=[
                pl.BlockSpec(memory_space=pltpu.SMEM),
                pl.BlockSpec((1, 1, width), lambda b, l: (b, 0, 1)),
                pl.BlockSpec((None, None, N_HEADS_B, DK_B, DV_B), lambda b, l: (l[0], b, 0, 0, 0)),
                pl.BlockSpec((None, 1, WIDTH_BV), lambda b, l: (l[0], 0, 0)),
                pl.BlockSpec((None, 1, WIDTH_BV), lambda b, l: (l[0], 0, 0)),
            ],
            out_specs=[
                pl.BlockSpec((1, 1, WIDTH_BV), lambda b, l: (b, 0, 0)),
                pl.BlockSpec((None, N_HEADS_B, DK_B, DV_B), lambda b, l: (b, 0, 0, 0)),
            ],
        ),
        out_shape=[jax.ShapeDtypeStruct((nb, 1, WIDTH_BV), BF16),
                   jax.ShapeDtypeStruct((nb, N_HEADS_B, DK_B, DV_B), F32)],
        compiler_params=pltpu.CompilerParams(
            dimension_semantics=("parallel",), vmem_limit_bytes=V7X_VMEM_LIMIT),
        name="ret_decode",
    )(layer, log_g, z3, state_ret, gn_g, gn_b)


def kernel(x_prompt, x_sample, cache_k, cache_v, state_ret, p_prompt, p_sample, w_in, w_out, gn_g,
           gn_b, ffn1_w1, ffn1_w3, ffn1_w2, ffn2_w1, ffn2_w3, ffn2_w2, w_ple, w_gate, ln_g, ln_b):
    batch, seq, _ = x_prompt.shape
    dec_batch, dec_seq, _ = x_sample.shape
    assert dec_seq == 1 and seq == BAND * DILATIONS[-1][1]
    assert cache_k.shape[2] == BAND * DILATIONS[-1][1]
    m_p, m_s = batch * seq, dec_batch * dec_seq
    tm_p = 1024
    assert m_p % tm_p == 0

    f1 = (ffn1_w1, ffn1_w3, ffn1_w2)
    f2 = (ffn2_w1, ffn2_w3, ffn2_w2)
    w_in_b = w_in.astype(BF16)
    w_kv_b = w_in[:, :, WIDTH_A:3 * WIDTH_A].astype(BF16)
    ln_g4 = ln_g.reshape(DEPTH, 4, 1, D_MODEL)
    ln_b4 = ln_b.reshape(DEPTH, 4, 1, D_MODEL)
    gn_g3 = gn_g.reshape(DEPTH, 1, WIDTH_BV)
    gn_b3 = gn_b.reshape(DEPTH, 1, WIDTH_BV)
    p_p = p_prompt.reshape(DEPTH, m_p, PLE_DIM)
    p_s = p_sample.reshape(DEPTH, m_s, PLE_DIM)

    heads_a = jnp.arange(1, N_HEADS_A + 1, dtype=F32)
    slopes = jnp.exp2(-8.0 * heads_a / N_HEADS_A)
    log_g = jnp.log(1.0 - jnp.exp2(-5.0 - jnp.arange(N_HEADS_B, dtype=F32)))

    def layer_step(carry, i):
        xp, xs, k_p, v_p, f1_b = carry
        layer = i.reshape(1)
        nxt = jnp.minimum(i + 1, DEPTH - 1)

        xp, xs, *f2_b = _ffn_ln(jnp.stack([i, i]), xp, xs, *f1_b, *f2, ln_g4, ln_b4, 0, tm_p)

        zp = _in_proj(layer, xp, w_in_b, tm_p, WIDTH_A, 4, skip_after_first=2)
        k_p, v_p, kv_p = _in_proj_kv(layer, xp, w_kv_b, k_p, v_p, 512)
        zs = _in_proj(layer, xs, w_in_b, m_s, WIDTH_A, IN_WIDTH // WIDTH_A)

        oa_p = _attn_prompt(layer, zp, kv_p, slopes, batch, seq)
        ob_p, ret_p = _ret_prompt(layer, zp, log_g, gn_g3, gn_b3, batch, seq)
        zs3 = zs.reshape(m_s, 1, IN_WIDTH)
        oa_s = _attn_decode(layer, slopes, zs3, cache_k, cache_v).reshape(m_s, WIDTH_A)
        ob_s, ret_s = _ret_decode(layer, log_g, zs3, state_ret, gn_g3, gn_b3)
        ob_s = ob_s.reshape(m_s, WIDTH_BV)

        xp = _out_proj_ln(layer, oa_p, ob_p, xp, w_out, ln_g4, ln_b4, 512)
        xs = _out_proj_ln(layer, oa_s, ob_s, xs, w_out, ln_g4, ln_b4, m_s)

        xp, xs, *f1_b = _ffn_ln(jnp.stack([i, nxt]), xp, xs, *f2_b, *f1, ln_g4, ln_b4, 2, tm_p)

        xp = _ple_ln(layer, xp, p_p, w_ple, w_gate, ln_g4, ln_b4, 512)
        xs = _ple_ln(layer, xs, p_s, w_ple, w_gate, ln_g4, ln_b4, m_s)

        k_s = zs[:, WIDTH_A:2 * WIDTH_A].reshape(dec_batch, dec_seq, N_HEADS_A, HEAD_DIM)
        v_s = zs[:, 2 * WIDTH_A:3 * WIDTH_A].reshape(dec_batch, dec_seq, N_HEADS_A, HEAD_DIM)
        return (xp, xs, k_p, v_p, tuple(f1_b)), (ret_p, k_s, v_s, ret_s)

    kv_shape = (DEPTH, m_p, N_HEADS_A, HEAD_DIM)
    (xp, xs, k_p, v_p, _), (ret_p, k_s, v_s, ret_s) = lax.scan(
        layer_step,
        (x_prompt.reshape(m_p, D_MODEL), x_sample.reshape(m_s, D_MODEL),
         jnp.zeros(kv_shape, F32), jnp.zeros(kv_shape, F32),
         tuple(w[0].astype(BF16) for w in f1)),
        jnp.arange(DEPTH, dtype=jnp.int32))
    kv_out = (DEPTH, batch, seq, N_HEADS_A, HEAD_DIM)
    return (xp.reshape(batch, seq, D_MODEL), xs.reshape(dec_batch, dec_seq, D_MODEL),
            k_p.reshape(kv_out), v_p.reshape(kv_out), ret_p, k_s, v_s, ret_s)
```

```python
import functools

import jax
import jax.numpy as jnp
from jax import lax
from jax.experimental import pallas as pl
from jax.experimental.pallas import tpu as pltpu

F32 = jnp.float32
BF16 = jnp.bfloat16

D_MODEL = 2048
DEPTH = 4
HEAD_DIM = 128
WIDTH_A = 1024
N_HEADS_A = 8
DILATIONS = ((128, 1), (512, 4), (2048, 16))
N_HEADS_B = 8
DV_B = 128
DK_B = 64
WIDTH_BQK = N_HEADS_B * DK_B
WIDTH_BV = N_HEADS_B * DV_B
IN_WIDTH = 3 * WIDTH_A + 2 * WIDTH_BQK + 2 * WIDTH_BV
FFN_DIM = 5632
PLE_DIM = 256
RET_CHUNK = 128
LN_EPS = 1e-5
GN_EPS = 1e-5
NEG_INF = -1e30
DEEPNORM_ALPHA = (2 * DEPTH) ** 0.25
ATTN_SCALE = HEAD_DIM ** -0.5
KB_SCALE = DK_B ** -0.5

BAND = 128
V7X_VMEM_LIMIT = 56 * 1024 * 1024
V7X_VMEM_LIMIT_FFN = 63 * 1024 * 1024
FFN_TILE_F = 512
LN_ROWS = 128
ATTN_UNROLL = 16
ATTN_GROUP = 4
RET_BATCH = 2
LOG2E = 1.4426950408889634
LN2 = 0.6931471805599453


def _dot(a, b):
    return jnp.dot(a, b, preferred_element_type=F32)


def _dot_nt(a, b):
    return lax.dot_general(a, b, (((1,), (1,)), ((), ())), preferred_element_type=F32)


def _layer_norm_rows(y, g, b):
    mu = jnp.mean(y, axis=-1, keepdims=True)
    d = y - mu
    var = jnp.mean(d * d, axis=-1, keepdims=True)
    return d * lax.rsqrt(var + LN_EPS) * g + b


def _residual_ln_epilogue(x_ref, o_ref, g_ref, b_ref, branch_scale):
    rows = o_ref.shape[0]
    step = min(rows, LN_ROWS)
    g = g_ref[...]
    b = b_ref[...]

    def body(c, carry):
        r = pl.multiple_of(c * step, step)
        y = DEEPNORM_ALPHA * x_ref[pl.ds(r, step), :] + branch_scale * o_ref[pl.ds(r, step), :]
        o_ref[pl.ds(r, step), :] = _layer_norm_rows(y, g, b)
        return carry

    lax.fori_loop(0, rows // step, body, 0)


def _swiglu_step(xb, w1b, w3b, w2b):
    h1 = _dot(xb, w1b)
    h3 = _dot(xb, w3b)
    act = (h1 * jax.nn.sigmoid(h1) * h3).astype(BF16)
    return _dot(act, w2b)


def _ffn_kernel(l_ref, x_ref, xs_ref, w1_ref, w3_ref, w2_ref, c1_ref, c3_ref, c2_ref, *rest, nf,
                n_in_blocks):
    if n_in_blocks:
        cin_ref, g_ref, b_ref, o_ref, os_ref, n1_ref, n3_ref, n2_ref, nin_ref, xb_ref, xsb_ref = rest
    else:
        g_ref, b_ref, o_ref, os_ref, n1_ref, n3_ref, n2_ref, xb_ref, xsb_ref = rest
    i = pl.program_id(0)
    f = pl.program_id(1)

    @pl.when(f == 0)
    def _():
        xb_ref[...] = x_ref[...].astype(BF16)
        o_ref[...] = jnp.zeros_like(o_ref)

    @pl.when((f == 0) & (i == 0))
    def _():
        xsb_ref[...] = xs_ref[...].astype(BF16)
        os_ref[...] = jnp.zeros_like(os_ref)

    n1_ref[...] = c1_ref[...].astype(BF16)
    n3_ref[...] = c3_ref[...].astype(BF16)
    n2_ref[...] = c2_ref[...].astype(BF16)
    if n_in_blocks:
        @pl.when(f < n_in_blocks)
        def _():
            nin_ref[...] = cin_ref[...].astype(BF16)

    o_ref[...] += _swiglu_step(xb_ref[...], w1_ref[...], w3_ref[...], w2_ref[...])

    @pl.when(i == 0)
    def _():
        os_ref[...] += _swiglu_step(xsb_ref[...], w1_ref[...], w3_ref[...], w2_ref[...])

    @pl.when(f == nf - 1)
    def _():
        _residual_ln_epilogue(x_ref, o_ref, g_ref, b_ref, 0.5)

    @pl.when((f == nf - 1) & (i == 0))
    def _():
        _residual_ln_epilogue(xs_ref, os_ref, g_ref, b_ref, 0.5)


def _ffn_ln(layers, x, xs, w1, w3, w2, c1, c3, c2, ln_g, ln_b, ln_idx, tm, c_in=None):
    m, ms = x.shape[0], xs.shape[0]
    nf = FFN_DIM // FFN_TILE_F
    cr = D_MODEL // (m // tm)
    n_in_blocks = 0 if c_in is None else IN_WIDTH // WIDTH_A
    assert n_in_blocks <= nf
    in_block = lambda f: jnp.minimum(f, n_in_blocks - 1)
    in_specs_in = [] if c_in is None else [
        pl.BlockSpec((None, cr, WIDTH_A), lambda i, f, l: (l[0], i, in_block(f)))]
    out_specs_in = [] if c_in is None else [
        pl.BlockSpec((cr, WIDTH_A), lambda i, f, l: (i, in_block(f)))]
    out_shape_in = [] if c_in is None else [jax.ShapeDtypeStruct((D_MODEL, IN_WIDTH), BF16)]
    args_in = [] if c_in is None else [c_in]
    return pl.pallas_call(
        functools.partial(_ffn_kernel, nf=nf, n_in_blocks=n_in_blocks),
        grid_spec=pltpu.PrefetchScalarGridSpec(
            num_scalar_prefetch=1,
            grid=(m // tm, nf),
            in_specs=[
                pl.BlockSpec((tm, D_MODEL), lambda i, f, l: (i, 0)),
                pl.BlockSpec((ms, D_MODEL), lambda i, f, l: (0, 0)),
                pl.BlockSpec((D_MODEL, FFN_TILE_F), lambda i, f, l: (0, f)),
                pl.BlockSpec((D_MODEL, FFN_TILE_F), lambda i, f, l: (0, f)),
                pl.BlockSpec((FFN_TILE_F, D_MODEL), lambda i, f, l: (f, 0)),
                pl.BlockSpec((None, cr, FFN_TILE_F), lambda i, f, l: (l[1], i, f)),
                pl.BlockSpec((None, cr, FFN_TILE_F), lambda i, f, l: (l[1], i, f)),
                pl.BlockSpec((None, FFN_TILE_F, cr), lambda i, f, l: (l[1], f, i)),
            ] + in_specs_in + [
                pl.BlockSpec((None, None, 1, D_MODEL), lambda i, f, l: (l[0], ln_idx, 0, 0)),
                pl.BlockSpec((None, None, 1, D_MODEL), lambda i, f, l: (l[0], ln_idx, 0, 0)),
            ],
            out_specs=[
                pl.BlockSpec((tm, D_MODEL), lambda i, f, l: (i, 0)),
                pl.BlockSpec((ms, D_MODEL), lambda i, f, l: (0, 0)),
                pl.BlockSpec((cr, FFN_TILE_F), lambda i, f, l: (i, f)),
                pl.BlockSpec((cr, FFN_TILE_F), lambda i, f, l: (i, f)),
                pl.BlockSpec((FFN_TILE_F, cr), lambda i, f, l: (f, i)),
            ] + out_specs_in,
            scratch_shapes=[pltpu.VMEM((tm, D_MODEL), BF16), pltpu.VMEM((ms, D_MODEL), BF16)],
        ),
        out_shape=[jax.ShapeDtypeStruct((m, D_MODEL), F32),
                   jax.ShapeDtypeStruct((ms, D_MODEL), F32),
                   jax.ShapeDtypeStruct((D_MODEL, FFN_DIM), BF16),
                   jax.ShapeDtypeStruct((D_MODEL, FFN_DIM), BF16),
                   jax.ShapeDtypeStruct((FFN_DIM, D_MODEL), BF16)] + out_shape_in,
        compiler_params=pltpu.CompilerParams(
            dimension_semantics=("arbitrary", "arbitrary"), vmem_limit_bytes=V7X_VMEM_LIMIT_FFN),
        name="ffn_ln",
    )(layers, x, xs, w1, w3, w2, c1, c3, c2, *args_in, ln_g, ln_b)


def _proj_kernel(l_ref, x_ref, w_ref, o_ref, xb_ref):
    @pl.when(pl.program_id(1) == 0)
    def _():
        xb_ref[...] = x_ref[...].astype(BF16)

    o_ref[...] = _dot(xb_ref[...], w_ref[...])


def _in_proj(layer, x, w_in, tm, tn, n_blocks, skip_after_first=0):
    m = x.shape[0]
    col = lambda n: n + jnp.where(n > 0, skip_after_first, 0)
    return pl.pallas_call(
        _proj_kernel,
        grid_spec=pltpu.PrefetchScalarGridSpec(
            num_scalar_prefetch=1,
            grid=(m // tm, n_blocks),
            in_specs=[
                pl.BlockSpec((tm, D_MODEL), lambda i, n, l: (i, 0)),
                pl.BlockSpec((D_MODEL, tn), lambda i, n, l: (0, col(n))),
            ],
            out_specs=pl.BlockSpec((tm, tn), lambda i, n, l: (i, n)),
            scratch_shapes=[pltpu.VMEM((tm, D_MODEL), BF16)],
        ),
        out_shape=jax.ShapeDtypeStruct((m, n_blocks * tn), F32),
        compiler_params=pltpu.CompilerParams(
            dimension_semantics=("parallel", "arbitrary"), vmem_limit_bytes=V7X_VMEM_LIMIT),
        name="in_proj",
    )(layer, x, w_in)


def _proj_kv_kernel(l_ref, x_ref, wk_ref, wv_ref, k_in_ref, v_in_ref, k_ref, v_ref, kv_ref):
    xb = x_ref[...].astype(BF16)
    for n, (w_ref, dst_ref) in enumerate(((wk_ref, k_ref), (wv_ref, v_ref))):
        kv_ref[:, n * WIDTH_A:(n + 1) * WIDTH_A] = _dot(xb, w_ref[...])
        for h in range(N_HEADS_A):
            c0 = n * WIDTH_A + h * HEAD_DIM
            dst_ref[:, h, :] = kv_ref[:, c0:c0 + HEAD_DIM]


def _in_proj_kv(layer, x, w_in, k_all, v_all, tm):
    m = x.shape[0]
    out_spec = pl.BlockSpec((None, tm, N_HEADS_A, HEAD_DIM), lambda i, l: (l[0], i, 0, 0))
    return pl.pallas_call(
        _proj_kv_kernel,
        grid_spec=pltpu.PrefetchScalarGridSpec(
            num_scalar_prefetch=1,
            grid=(m // tm,),
            in_specs=[
                pl.BlockSpec((tm, D_MODEL), lambda i, l: (i, 0)),
                pl.BlockSpec((D_MODEL, WIDTH_A), lambda i, l: (0, 1)),
                pl.BlockSpec((D_MODEL, WIDTH_A), lambda i, l: (0, 2)),
                pl.BlockSpec(memory_space=pl.ANY),
                pl.BlockSpec(memory_space=pl.ANY),
            ],
            out_specs=[out_spec, out_spec, pl.BlockSpec((tm, 2 * WIDTH_A), lambda i, l: (i, 0))],
        ),
        out_shape=[jax.ShapeDtypeStruct(k_all.shape, F32), jax.ShapeDtypeStruct(v_all.shape, F32),
                   jax.ShapeDtypeStruct((m, 2 * WIDTH_A), F32)],
        input_output_aliases={4: 0, 5: 1},
        compiler_params=pltpu.CompilerParams(
            dimension_semantics=("parallel",), vmem_limit_bytes=V7X_VMEM_LIMIT),
        name="in_proj_kv",
    )(layer, x, w_in, w_in, k_all, v_all)


def _out_proj_kernel(l_ref, oa_ref, ob_ref, x_ref, w_ref, g_ref, b_ref, o_ref, wb_ref):
    @pl.when(pl.program_id(0) == 0)
    def _():
        wb_ref[...] = w_ref[...].astype(BF16)

    o_ref[...] = _dot(oa_ref[...], wb_ref[0:WIDTH_A, :]) + _dot(ob_ref[...], wb_ref[WIDTH_A:, :])
    _residual_ln_epilogue(x_ref, o_ref, g_ref, b_ref, 1.0)


def _out_proj_ln(layer, oa, ob, x, w_out, ln_g, ln_b, tm):
    m = x.shape[0]
    return pl.pallas_call(
        _out_proj_kernel,
        grid_spec=pltpu.PrefetchScalarGridSpec(
            num_scalar_prefetch=1,
            grid=(m // tm,),
            in_specs=[
                pl.BlockSpec((tm, WIDTH_A), lambda i, l: (i, 0)),
                pl.BlockSpec((tm, WIDTH_BV), lambda i, l: (i, 0)),
                pl.BlockSpec((tm, D_MODEL), lambda i, l: (i, 0)),
                pl.BlockSpec((None, WIDTH_A + WIDTH_BV, D_MODEL), lambda i, l: (l[0], 0, 0),
                             pipeline_mode=pl.Buffered(1)),
                pl.BlockSpec((None, None, 1, D_MODEL), lambda i, l: (l[0], 1, 0, 0)),
                pl.BlockSpec((None, None, 1, D_MODEL), lambda i, l: (l[0], 1, 0, 0)),
            ],
            out_specs=pl.BlockSpec((tm, D_MODEL), lambda i, l: (i, 0)),
            scratch_shapes=[pltpu.VMEM((WIDTH_A + WIDTH_BV, D_MODEL), BF16)],
        ),
        out_shape=jax.ShapeDtypeStruct((m, D_MODEL), F32),
        compiler_params=pltpu.CompilerParams(
            dimension_semantics=("arbitrary",), vmem_limit_bytes=V7X_VMEM_LIMIT),
        name="out_proj_ln",
    )(layer, oa, ob, x, w_out, ln_g, ln_b)


def _ple_kernel(l_ref, x_ref, p_ref, wp_ref, wg_ref, g_ref, b_ref, o_ref, wpb_ref, wgb_ref):
    @pl.when(pl.program_id(0) == 0)
    def _():
        wpb_ref[...] = wp_ref[...].astype(BF16)
        wgb_ref[...] = wg_ref[...].astype(BF16)

    gate = jax.nn.sigmoid(_dot(x_ref[...].astype(BF16), wgb_ref[...]))
    o_ref[...] = _dot(p_ref[...].astype(BF16), wpb_ref[...]) * gate
    _residual_ln_epilogue(x_ref, o_ref, g_ref, b_ref, 1.0)


def _ple_ln(layer, x, p, w_ple, w_gate, ln_g, ln_b, tm):
    m = x.shape[0]
    return pl.pallas_call(
        _ple_kernel,
        grid_spec=pltpu.PrefetchScalarGridSpec(
            num_scalar_prefetch=1,
            grid=(m // tm,),
            in_specs=[
                pl.BlockSpec((tm, D_MODEL), lambda i, l: (i, 0)),
                pl.BlockSpec((None, tm, PLE_DIM), lambda i, l: (l[0], i, 0)),
                pl.BlockSpec((None, PLE_DIM, D_MODEL), lambda i, l: (l[0], 0, 0),
                             pipeline_mode=pl.Buffered(1)),
                pl.BlockSpec((None, D_MODEL, D_MODEL), lambda i, l: (l[0], 0, 0),
                             pipeline_mode=pl.Buffered(1)),
                pl.BlockSpec((None, None, 1, D_MODEL), lambda i, l: (l[0], 3, 0, 0)),
                pl.BlockSpec((None, None, 1, D_MODEL), lambda i, l: (l[0], 3, 0, 0)),
            ],
            out_specs=pl.BlockSpec((tm, D_MODEL), lambda i, l: (i, 0)),
            scratch_shapes=[pltpu.VMEM((PLE_DIM, D_MODEL), BF16),
                            pltpu.VMEM((D_MODEL, D_MODEL), BF16)],
        ),
        out_shape=jax.ShapeDtypeStruct((m, D_MODEL), F32),
        compiler_params=pltpu.CompilerParams(
            dimension_semantics=("arbitrary",), vmem_limit_bytes=V7X_VMEM_LIMIT),
        name="ple_ln",
    )(layer, x, p, w_ple, w_gate, ln_g, ln_b)


def _merge_pair(o_a, l_a, o_b, l_b):
    m = jnp.maximum(l_a, l_b)
    e_a = jnp.exp(l_a - m)
    e_b = jnp.exp(l_b - m)
    den = e_a + e_b
    return (e_a * o_a + e_b * o_b) / den, m + jnp.log(den)


def _attn_prompt_kernel(l_ref, slope_ref, q_ref, k_ref, v_ref, o_ref,
                        x4_ref, x16_ref, va_ref, s_ref, p_ref, ls_ref, bias_ref, og_ref, lse_ref,
                        *, seq):
    nblk = seq // BAND
    slab = ATTN_GROUP * BAND
    part4 = seq // 4
    slope = slope_ref[pl.program_id(1)]

    nat = (q_ref, k_ref, v_ref)
    for t in range(3):
        for a in range(4):
            for c in range(part4 // BAND):
                x4_ref[t, pl.ds(a * part4 + c * BAND, BAND), :] = (
                    nat[t][pl.ds(a + 4 * c * BAND, BAND, stride=4), :])
    for t in range(3):
        for a in range(4):
            for b in range(4):
                x16_ref[t, pl.ds((a + 4 * b) * BAND, BAND), :] = (
                    x4_ref[t, pl.ds(a * part4 + b, BAND, stride=4), :])

    ones = jnp.ones((slab, HEAD_DIM), BF16)
    for slot, src in enumerate((lambda r: v_ref[r, :], lambda r: x4_ref[2, r, :],
                                lambda r: x16_ref[2, r, :])):
        for c in range(seq // slab):
            r = pl.ds(c * slab, slab)
            va_ref[slot, r, 0:HEAD_DIM] = src(r).astype(BF16)
            va_ref[slot, r, HEAD_DIM:2 * HEAD_DIM] = ones

    row = lax.broadcasted_iota(jnp.int32, (slab, 2 * BAND), 0)
    col = lax.broadcasted_iota(jnp.int32, (slab, 2 * BAND), 1)
    dist = jnp.bitwise_and(row, BAND - 1) + BAND - col
    in_band = (dist >= 0) & (dist <= BAND)
    starts_class = (row < BAND) & (col < BAND)
    distf = dist.astype(F32)
    tile1 = jnp.where(in_band, -(slope * (1 * LOG2E)) * distf, NEG_INF)
    bias_ref[1] = tile1
    bias_ref[0] = jnp.where(starts_class, NEG_INF, tile1)
    bias_ref[2] = jnp.where(starts_class | ~in_band, NEG_INF, -(slope * (4 * LOG2E)) * distf)
    dist_own = dist[:, BAND:]
    bias_ref[3, :, 0:BAND] = jnp.where(dist_own >= 0,
                                       -(slope * (16 * LOG2E)) * dist_own.astype(F32), NEG_INF)

    def band_pass(qf, kf, vf, two, first_tile, rest_tile, store):
        kw = 2 * BAND if two else BAND

        if two:
            s_ref[0:BAND, 0:BAND] = jnp.zeros((BAND, BAND), F32)
            s_ref[0:BAND, BAND:kw] = _dot_nt(qf(0, BAND).astype(BF16), kf(0, BAND).astype(BF16))

            def scores(j, carry):
                r = pl.multiple_of(j * BAND, BAND)
                rp = pl.multiple_of(r - BAND, BAND)
                s_ref[pl.ds(r, BAND), :] = _dot_nt(qf(r, BAND).astype(BF16),
                                                   kf(rp, kw).astype(BF16))
                return carry

            lax.fori_loop(1, nblk, scores, 0, unroll=ATTN_UNROLL)
        else:
            def scores(j, carry):
                r = pl.multiple_of(j * BAND, BAND)
                s_ref[pl.ds(r, BAND), 0:kw] = _dot_nt(qf(r, BAND).astype(BF16),
                                                      kf(r, BAND).astype(BF16))
                return carry

            lax.fori_loop(0, nblk, scores, 0, unroll=ATTN_UNROLL)

        def softmax_slab(r, tile):
            s = s_ref[pl.ds(r, slab), 0:kw] * (ATTN_SCALE * LOG2E) + bias_ref[tile, :, 0:kw]
            m = jnp.max(s, axis=-1, keepdims=True)
            p = jnp.exp2(s - m)
            p_ref[pl.ds(r, slab), 0:kw] = p.astype(BF16)
            ls_ref[pl.ds(r, slab), :] = jnp.broadcast_to(m * LN2, (slab, HEAD_DIM))

        first_rest = 0
        if first_tile != rest_tile:
            softmax_slab(0, first_tile)
            first_rest = 1

        def rest_slab(g, carry):
            softmax_slab(pl.multiple_of(g * slab, slab), rest_tile)
            return carry

        lax.fori_loop(first_rest, nblk // ATTN_GROUP, rest_slab, 0, unroll=True)

        def finish(j, r, o_l):
            l = o_l[:, HEAD_DIM:]
            store(j, o_l[:, 0:HEAD_DIM] / l, ls_ref[pl.ds(r, BAND), :] + jnp.log(l))

        if two:
            finish(0, 0, _dot(p_ref[0:BAND, BAND:kw], vf(0, BAND)))

            def values(j, carry):
                r = pl.multiple_of(j * BAND, BAND)
                rp = pl.multiple_of(r - BAND, BAND)
                finish(j, r, _dot(p_ref[pl.ds(r, BAND), :], vf(rp, kw)))
                return carry

            lax.fori_loop(1, nblk, values, 0, unroll=ATTN_UNROLL)
        else:
            def values(j, carry):
                r = pl.multiple_of(j * BAND, BAND)
                finish(j, r, _dot(p_ref[pl.ds(r, BAND), 0:kw], vf(r, BAND)))
                return carry

            lax.fori_loop(0, nblk, values, 0, unroll=ATTN_UNROLL)

    def store_contiguous(slot):
        def store(j, o, lse):
            og_ref[slot, pl.ds(j * BAND, BAND), :] = o
            lse_ref[slot, pl.ds(j * BAND, BAND), :] = lse
        return store

    def store_16_as_4(j, o, lse):
        start = jnp.bitwise_and(j, 3) * part4 + jnp.right_shift(j, 2)
        og_ref[2, pl.ds(start, BAND, stride=4), :] = o
        lse_ref[2, pl.ds(start, BAND, stride=4), :] = lse

    values_of = lambda slot: (lambda start, size: va_ref[slot, pl.ds(start, size), :])
    natural = [lambda start, size, ref=ref: ref[pl.ds(start, size), :] for ref in nat[:2]]
    by4 = [lambda start, size, t=t: x4_ref[t, pl.ds(start, size), :] for t in range(2)]
    by16 = [lambda start, size, t=t: x16_ref[t, pl.ds(start, size), :] for t in range(2)]
    natural.append(values_of(0))
    by4.append(values_of(1))
    by16.append(values_of(2))
    band_pass(*natural, True, 0, 1, store_contiguous(0))
    band_pass(*by4, True, 2, 2, store_contiguous(1))
    band_pass(*by16, False, 3, 3, store_16_as_4)

    for a in range(4):
        for c in range(part4 // BAND):
            src = pl.ds(a * part4 + c * BAND, BAND)
            dst = pl.ds(a + 4 * c * BAND, BAND, stride=4)
            o, lse = _merge_pair(og_ref[1, src, :], lse_ref[1, src, :],
                                 og_ref[2, src, :], lse_ref[2, src, :])
            og_ref[3, dst, :] = o
            lse_ref[3, dst, :] = lse

    def final_merge(c, carry):
        rows = pl.ds(pl.multiple_of(c * BAND, BAND), BAND)
        o, _ = _merge_pair(og_ref[0, rows, :], lse_ref[0, rows, :],
                           og_ref[3, rows, :], lse_ref[3, rows, :])
        o_ref[rows, :] = o.astype(BF16)
        return carry

    lax.fori_loop(0, nblk, final_merge, 0, unroll=True)


def _attn_prompt(layer, zqb, kv, slopes, batch, seq):
    slab = ATTN_GROUP * BAND
    return pl.pallas_call(
        functools.partial(_attn_prompt_kernel, seq=seq),
        grid_spec=pltpu.PrefetchScalarGridSpec(
            num_scalar_prefetch=1,
            grid=(batch, N_HEADS_A),
            in_specs=[
                pl.BlockSpec(memory_space=pltpu.SMEM),
                pl.BlockSpec((seq, HEAD_DIM), lambda b, h, l: (b, h)),
                pl.BlockSpec((seq, HEAD_DIM), lambda b, h, l: (b, h)),
                pl.BlockSpec((seq, HEAD_DIM), lambda b, h, l: (b, N_HEADS_A + h)),
            ],
            out_specs=pl.BlockSpec((seq, HEAD_DIM), lambda b, h, l: (b, h)),
            scratch_shapes=[
                pltpu.VMEM((3, seq, HEAD_DIM), F32),
                pltpu.VMEM((3, seq, HEAD_DIM), F32),
                pltpu.VMEM((3, seq, 2 * HEAD_DIM), BF16),
                pltpu.VMEM((seq, 2 * BAND), F32),
                pltpu.VMEM((seq, 2 * BAND), BF16),
                pltpu.VMEM((seq, HEAD_DIM), F32),
                pltpu.VMEM((4, slab, 2 * BAND), F32),
                pltpu.VMEM((4, seq, HEAD_DIM), F32),
                pltpu.VMEM((4, seq, HEAD_DIM), F32),
            ],
        ),
        out_shape=jax.ShapeDtypeStruct((batch * seq, WIDTH_A), BF16),
        compiler_params=pltpu.CompilerParams(
            dimension_semantics=("parallel", "parallel"), vmem_limit_bytes=V7X_VMEM_LIMIT),
        name="attn_prompt",
    )(layer, slopes, zqb, kv, kv)


def _group_norm_gate(o, gate, gn_g, gn_b):
    mu = jnp.mean(o, axis=-1, keepdims=True)
    d = o - mu
    var = jnp.mean(d * d, axis=-1, keepdims=True)
    on = d * lax.rsqrt(var + GN_EPS)
    return (on * gn_g + gn_b) * (gate * jax.nn.sigmoid(gate))


def _ret_prompt_kernel(l_ref, lg_ref, q_ref, k_ref, v_ref, gate_ref, gng_ref, gnb_ref, o_ref,
                       s_ref, st_ref, *, seq):
    pair = pl.program_id(1)
    c_len = RET_CHUNK
    lane = lax.broadcasted_iota(jnp.int32, (c_len, 2 * DK_B), 1)
    pos_r = lax.broadcasted_iota(jnp.int32, (c_len, c_len), 0)
    pos_c = lax.broadcasted_iota(jnp.int32, (c_len, c_len), 1)
    rel = (pos_r - pos_c).astype(F32)
    pos = lax.broadcasted_iota(jnp.int32, (c_len, 1), 0).astype(F32)
    ones11 = jnp.ones((1, 1), F32)

    heads = []
    for hx in range(2):
        lg = lg_ref[2 * pair + hx]
        decay_mask = jnp.where(rel >= 0, jnp.exp(jnp.maximum(rel, 0.0) * lg), 0.0)
        q_decay = jnp.exp((pos + 1.0) * lg)
        k_decay = jnp.exp((c_len - 1.0 - pos) * lg)
        chunk_decay = jnp.exp((c_len * ones11) * lg)
        heads.append((decay_mask, q_decay, k_decay, chunk_decay))

    st_ref[...] = jnp.zeros_like(st_ref)

    def chunk(c, carry):
        units = []
        for bi in range(RET_BATCH):
            r = pl.multiple_of(bi * seq + c * c_len, c_len)
            q2 = q_ref[pl.ds(r, c_len), :]
            k2 = k_ref[pl.ds(r, c_len), :] * KB_SCALE
            kb = k2.astype(BF16)
            for hx in range(2):
                qx = jnp.where((lane >= DK_B) if hx else (lane < DK_B), q2, 0.0).astype(BF16)
                units.append((2 * bi + hx, hx, r, qx, kb, k2))
        inners = [_dot_nt(qx, kb) * heads[hx][0] for _, hx, _, qx, kb, _ in units]
        vxs = [v_ref[pl.ds(r, c_len), hx * DV_B:(hx + 1) * DV_B].astype(BF16)
               for _, hx, r, _, _, _ in units]
        states = [st_ref[u] for u, *_ in units]
        intras = [_dot(inner.astype(BF16), vx) for inner, vx in zip(inners, vxs)]
        crosses = [_dot(qx, state.astype(BF16)) for (_, _, _, qx, _, _), state in zip(units, states)]
        updates = [_dot((k2 * heads[hx][2]).T.astype(BF16), vx)
                   for (_, hx, _, _, _, k2), vx in zip(units, vxs)]
        for (u, hx, r, _, _, _), intra, cross, update, state in zip(units, intras, crosses, updates,
                                                                  states):
            _, q_decay, _, chunk_decay = heads[hx]
            cols = slice(hx * DV_B, (hx + 1) * DV_B)
            st_ref[u] = state * chunk_decay + update
            res = _group_norm_gate(intra + cross * q_decay, gate_ref[pl.ds(r, c_len), cols],
                                   gng_ref[:, cols], gnb_ref[:, cols])
            o_ref[pl.ds(r, c_len), cols] = res.astype(BF16)
        return carry

    lax.fori_loop(0, seq // c_len, chunk, 0, unroll=4)
    for bi in range(RET_BATCH):
        s_ref[bi, 0] = st_ref[2 * bi, 0:DK_B, :]
        s_ref[bi, 1] = st_ref[2 * bi + 1, DK_B:2 * DK_B, :]


def _ret_prompt(layer, z, log_g, gn_g, gn_b, batch, seq):
    assert batch % RET_BATCH == 0
    rows = RET_BATCH * seq
    qcol = WIDTH_A // (2 * DK_B)
    kcol = qcol + WIDTH_BQK // (2 * DK_B)
    vcol = (WIDTH_A + 2 * WIDTH_BQK) // (2 * DV_B)
    gcol = vcol + WIDTH_BV // (2 * DV_B)
    return pl.pallas_call(
        functools.partial(_ret_prompt_kernel, seq=seq),
        grid_spec=pltpu.PrefetchScalarGridSpec(
            num_scalar_prefetch=1,
            grid=(batch // RET_BATCH, N_HEADS_B // 2),
            in_specs=[
                pl.BlockSpec(memory_space=pltpu.SMEM),
                pl.BlockSpec((rows, 2 * DK_B), lambda b, p, l: (b, qcol + p)),
                pl.BlockSpec((rows, 2 * DK_B), lambda b, p, l: (b, kcol + p)),
                pl.BlockSpec((rows, 2 * DV_B), lambda b, p, l: (b, vcol + p)),
                pl.BlockSpec((rows, 2 * DV_B), lambda b, p, l: (b, gcol + p)),
                pl.BlockSpec((None, 1, 2 * DV_B), lambda b, p, l: (l[0], 0, p)),
                pl.BlockSpec((None, 1, 2 * DV_B), lambda b, p, l: (l[0], 0, p)),
            ],
            out_specs=[
                pl.BlockSpec((rows, 2 * DV_B), lambda b, p, l: (b, p)),
                pl.BlockSpec((RET_BATCH, 2, DK_B, DV_B), lambda b, p, l: (b, p, 0, 0)),
            ],
            scratch_shapes=[pltpu.VMEM((2 * RET_BATCH, 2 * DK_B, DV_B), F32)],
        ),
        out_shape=[jax.ShapeDtypeStruct((batch * seq, WIDTH_BV), BF16),
                   jax.ShapeDtypeStruct((batch, N_HEADS_B, DK_B, DV_B), F32)],
        compiler_params=pltpu.CompilerParams(
            dimension_semantics=("parallel", "parallel"), vmem_limit_bytes=V7X_VMEM_LIMIT),
        name="ret_prompt",
    )(layer, log_g, z, z, z, z, gn_g, gn_b)


def _attn_decode_kernel(l_ref, slope_ref, z_ref, k1_ref, k4_ref, k16_ref, v1_ref, v4_ref, v16_ref,
                        o_ref):
    row = z_ref[0]
    steps = (BAND - lax.broadcasted_iota(jnp.int32, (BAND, 1), 0)).astype(F32)
    views = ((k1_ref, v1_ref, 1), (k4_ref, v4_ref, 4), (k16_ref, v16_ref, 16))
    for h in range(N_HEADS_A):
        cols = slice(h * HEAD_DIM, (h + 1) * HEAD_DIM)
        qh = row[:, cols]
        k_new = row[:, WIDTH_A + h * HEAD_DIM:WIDTH_A + (h + 1) * HEAD_DIM]
        v_new = row[:, 2 * WIDTH_A + h * HEAD_DIM:2 * WIDTH_A + (h + 1) * HEAD_DIM]
        slope = slope_ref[h]
        s_new = jnp.sum(qh * k_new, axis=1, keepdims=True) * ATTN_SCALE
        outs, lses = [], []
        for k_ref, v_ref, dil in views:
            s = (jnp.sum(k_ref[:, h, :] * qh, axis=1, keepdims=True) * ATTN_SCALE
                 - slope * (steps * dil))
            m = jnp.maximum(jnp.max(s, axis=0, keepdims=True), s_new)
            e = jnp.exp(s - m)
            e_new = jnp.exp(s_new - m)
            l = jnp.sum(e, axis=0, keepdims=True) + e_new
            o = (jnp.sum(e * v_ref[:, h, :], axis=0, keepdims=True) + e_new * v_new) / l
            outs.append(o)
            lses.append(m + jnp.log(l))
        m = jnp.maximum(jnp.maximum(lses[0], lses[1]), lses[2])
        w = [jnp.exp(x - m) for x in lses]
        o = (w[0] * outs[0] + w[1] * outs[1] + w[2] * outs[2]) / (w[0] + w[1] + w[2])
        o_ref[0, :, cols] = o.astype(BF16)


def _attn_decode(layer, slopes, z3, cache_k, cache_v):
    depth, nb, n_past = cache_k.shape[0], cache_k.shape[1], cache_k.shape[2]
    specs, args = [], []
    for cache in (cache_k, cache_v):
        for _, dil in DILATIONS:
            view = cache.reshape(depth, nb, n_past // dil, dil, N_HEADS_A, HEAD_DIM)
            last = n_past // dil // BAND - 1
            specs.append(pl.BlockSpec((None, None, BAND, None, N_HEADS_A, HEAD_DIM),
                                      lambda b, l, last=last: (l[0], b, last, 0, 0, 0)))
            args.append(view)
    return pl.pallas_call(
        _attn_decode_kernel,
        grid_spec=pltpu.PrefetchScalarGridSpec(
            num_scalar_prefetch=1,
            grid=(nb,),
            in_specs=[
                pl.BlockSpec(memory_space=pltpu.SMEM),
                pl.BlockSpec((1, 1, 3 * WIDTH_A), lambda b, l: (b, 0, 0)),
            ] + specs,
            out_specs=pl.BlockSpec((1, 1, WIDTH_A), lambda b, l: (b, 0, 0)),
        ),
        out_shape=jax.ShapeDtypeStruct((nb, 1, WIDTH_A), BF16),
        compiler_params=pltpu.CompilerParams(
            dimension_semantics=("parallel",), vmem_limit_bytes=V7X_VMEM_LIMIT),
        name="attn_decode",
    )(layer, slopes, z3, *args)


def _ret_decode_kernel(l_ref, lg_ref, z_ref, st_ref, gng_ref, gnb_ref, o_ref, sn_ref):
    row = z_ref[0]
    eye = (lax.broadcasted_iota(jnp.int32, (DK_B, DK_B), 0)
           == lax.broadcasted_iota(jnp.int32, (DK_B, DK_B), 1))
    ones11 = jnp.ones((1, 1), F32)
    v0 = 2 * WIDTH_BQK
    g0 = v0 + WIDTH_BV
    for h in range(N_HEADS_B):
        qh = row[:, h * DK_B:(h + 1) * DK_B]
        kh = row[:, WIDTH_BQK + h * DK_B:WIDTH_BQK + (h + 1) * DK_B] * KB_SCALE
        vh = row[:, v0 + h * DV_B:v0 + (h + 1) * DV_B]
        gh = row[:, g0 + h * DV_B:g0 + (h + 1) * DV_B]
        cols = slice(h * DV_B, (h + 1) * DV_B)
        gamma = jnp.exp(ones11 * lg_ref[h])
        q_col = jnp.sum(jnp.where(eye, qh, 0.0), axis=1, keepdims=True)
        k_col = jnp.sum(jnp.where(eye, kh, 0.0), axis=1, keepdims=True)
        state = st_ref[h]
        qk = jnp.sum(qh * kh, axis=1, keepdims=True)
        o = qk * vh + jnp.sum(q_col * state, axis=0, keepdims=True) * gamma
        sn_ref[h] = state * gamma + k_col * vh
        res = _group_norm_gate(o, gh, gng_ref[:, cols], gnb_ref[:, cols])
        o_ref[0, :, cols] = res.astype(BF16)


def _ret_decode(layer, log_g, z3, state_ret, gn_g, gn_b):
    nb = state_ret.shape[1]
    width = 2 * WIDTH_BQK + 2 * WIDTH_BV
    return pl.pallas_call(
        _ret_decode_kernel,
        grid_spec=pltpu.PrefetchScalarGridSpec(
            num_scalar_prefetch=1,
            grid=(nb,),
            in_specs=[
                pl.BlockSpec(memory_space=pltpu.SMEM),
                pl.BlockSpec((1, 1, width), lambda b, l: (b, 0, 1)),
                pl.BlockSpec((None, None, N_HEADS_B, DK_B, DV_B), lambda b, l: (l[0], b, 0, 0, 0)),
                pl.BlockSpec((None, 1, WIDTH_BV), lambda b, l: (l[0], 0, 0)),
                pl.BlockSpec((None, 1, WIDTH_BV), lambda b, l: (l[0], 0, 0)),
            ],
            out_specs=[
                pl.BlockSpec((1, 1, WIDTH_BV), lambda b, l: (b, 0, 0)),
                pl.BlockSpec((None, N_HEADS_B, DK_B, DV_B), lambda b, l: (b, 0, 0, 0)),
            ],
        ),
        out_shape=[jax.ShapeDtypeStruct((nb, 1, WIDTH_BV), BF16),
                   jax.ShapeDtypeStruct((nb, N_HEADS_B, DK_B, DV_B), F32)],
        compiler_params=pltpu.CompilerParams(
            dimension_semantics=("parallel",), vmem_limit_bytes=V7X_VMEM_LIMIT),
        name="ret_decode",
    )(layer, log_g, z3, state_ret, gn_g, gn_b)


def kernel(x_prompt, x_sample, cache_k, cache_v, state_ret, p_prompt, p_sample, w_in, w_out, gn_g,
           gn_b, ffn1_w1, ffn1_w3, ffn1_w2, ffn2_w1, ffn2_w3, ffn2_w2, w_ple, w_gate, ln_g, ln_b):
    batch, seq, _ = x_prompt.shape
    dec_batch, dec_seq, _ = x_sample.shape
    assert dec_seq == 1 and seq == BAND * DILATIONS[-1][1]
    assert cache_k.shape[2] == BAND * DILATIONS[-1][1]
    m_p, m_s = batch * seq, dec_batch * dec_seq
    tm_p = 1024
    assert m_p % tm_p == 0

    f1 = (ffn1_w1, ffn1_w3, ffn1_w2)
    f2 = (ffn2_w1, ffn2_w3, ffn2_w2)
    ln_g4 = ln_g.reshape(DEPTH, 4, 1, D_MODEL)
    ln_b4 = ln_b.reshape(DEPTH, 4, 1, D_MODEL)
    gn_g3 = gn_g.reshape(DEPTH, 1, WIDTH_BV)
    gn_b3 = gn_b.reshape(DEPTH, 1, WIDTH_BV)
    p_p = p_prompt.reshape(DEPTH, m_p, PLE_DIM)
    p_s = p_sample.reshape(DEPTH, m_s, PLE_DIM)

    heads_a = jnp.arange(1, N_HEADS_A + 1, dtype=F32)
    slopes = jnp.exp2(-8.0 * heads_a / N_HEADS_A)
    log_g = jnp.log(1.0 - jnp.exp2(-5.0 - jnp.arange(N_HEADS_B, dtype=F32)))

    def layer_step(carry, i):
        xp, xs, k_p, v_p, f1_b = carry
        layer = i.reshape(1)
        nxt = jnp.minimum(i + 1, DEPTH - 1)

        xp, xs, *f2_b, w_in_b = _ffn_ln(jnp.stack([i, i]), xp, xs, *f1_b, *f2, ln_g4, ln_b4, 0, tm_p,
                                        c_in=w_in)

        zp = _in_proj(layer, xp, w_in_b, tm_p, WIDTH_A, 4, skip_after_first=2)
        k_p, v_p, kv_p = _in_proj_kv(layer, xp, w_in_b, k_p, v_p, 512)
        zs = _in_proj(layer, xs, w_in_b, m_s, WIDTH_A, IN_WIDTH // WIDTH_A)

        oa_p = _attn_prompt(layer, zp, kv_p, slopes, batch, seq)
        ob_p, ret_p = _ret_prompt(layer, zp, log_g, gn_g3, gn_b3, batch, seq)
        zs3 = zs.reshape(m_s, 1, IN_WIDTH)
        oa_s = _attn_decode(layer, slopes, zs3, cache_k, cache_v).reshape(m_s, WIDTH_A)
        ob_s, ret_s = _ret_decode(layer, log_g, zs3, state_ret, gn_g3, gn_b3)
        ob_s = ob_s.reshape(m_s, WIDTH_BV)

        xp = _out_proj_ln(layer, oa_p, ob_p, xp, w_out, ln_g4, ln_b4, 512)
        xs = _out_proj_ln(layer, oa_s, ob_s, xs, w_out, ln_g4, ln_b4, m_s)

        xp, xs, *f1_b = _ffn_ln(jnp.stack([i, nxt]), xp, xs, *f2_b, *f1, ln_g4, ln_b4, 2, tm_p)

        xp = _ple_ln(layer, xp, p_p, w_ple, w_gate, ln_g4, ln_b4, 512)
        xs = _ple_ln(layer, xs, p_s, w_ple, w_gate, ln_g4, ln_b4, m_s)

        k_s = zs[:, WIDTH_A:2 * WIDTH_A].reshape(dec_batch, dec_seq, N_HEADS_A, HEAD_DIM)
        v_s = zs[:, 2 * WIDTH_A:3 * WIDTH_A].reshape(dec_batch, dec_seq, N_HEADS_A, HEAD_DIM)
        return (xp, xs, k_p, v_p, tuple(f1_b)), (ret_p, k_s, v_s, ret_s)

    kv_shape = (DEPTH, m_p, N_HEADS_A, HEAD_DIM)
    (xp, xs, k_p, v_p, _), (ret_p, k_s, v_s, ret_s) = lax.scan(
        layer_step,
        (x_prompt.reshape(m_p, D_MODEL), x_sample.reshape(m_s, D_MODEL),
         jnp.zeros(kv_shape, F32), jnp.zeros(kv_shape, F32),
         tuple(w[0].astype(BF16) for w in f1)),
        jnp.arange(DEPTH, dtype=jnp.int32))
    kv_out = (DEPTH, batch, seq, N_HEADS_A, HEAD_DIM)
    return (xp.reshape(batch, seq, D_MODEL), xs.reshape(dec_batch, dec_seq, D_MODEL),
            k_p.reshape(kv_out), v_p.reshape(kv_out), ret_p, k_s, v_s, ret_s)
```

```python
import functools

import jax
import jax.numpy as jnp
from jax import lax
from jax.experimental import pallas as pl
from jax.experimental.pallas import tpu as pltpu

F32 = jnp.float32
BF16 = jnp.bfloat16

D_MODEL = 2048
DEPTH = 4
HEAD_DIM = 128
WIDTH_A = 1024
N_HEADS_A = 8
DILATIONS = ((128, 1), (512, 4), (2048, 16))
N_HEADS_B = 8
DV_B = 128
DK_B = 64
WIDTH_BQK = N_HEADS_B * DK_B
WIDTH_BV = N_HEADS_B * DV_B
IN_WIDTH = 3 * WIDTH_A + 2 * WIDTH_BQK + 2 * WIDTH_BV
FFN_DIM = 5632
PLE_DIM = 256
RET_CHUNK = 128
LN_EPS = 1e-5
GN_EPS = 1e-5
NEG_INF = -1e30
DEEPNORM_ALPHA = (2 * DEPTH) ** 0.25
ATTN_SCALE = HEAD_DIM ** -0.5
KB_SCALE = DK_B ** -0.5

BAND = 128
V7X_VMEM_LIMIT = 56 * 1024 * 1024
V7X_VMEM_LIMIT_FFN = 63 * 1024 * 1024
FFN_TILE_F = 512
LN_ROWS = 128
ATTN_UNROLL = 16
ATTN_GROUP = 4
RET_BATCH = 2
LOG2E = 1.4426950408889634
LN2 = 0.6931471805599453


def _dot(a, b):
    return jnp.dot(a, b, preferred_element_type=F32)


def _dot_nt(a, b):
    return lax.dot_general(a, b, (((1,), (1,)), ((), ())), preferred_element_type=F32)


def _layer_norm_rows(y, g, b):
    mu = jnp.mean(y, axis=-1, keepdims=True)
    d = y - mu
    var = jnp.mean(d * d, axis=-1, keepdims=True)
    return d * lax.rsqrt(var + LN_EPS) * g + b


def _residual_ln_epilogue(x_ref, o_ref, g_ref, b_ref, branch_scale):
    rows = o_ref.shape[0]
    step = min(rows, LN_ROWS)
    g = g_ref[...]
    b = b_ref[...]

    def body(c, carry):
        r = pl.multiple_of(c * step, step)
        y = DEEPNORM_ALPHA * x_ref[pl.ds(r, step), :] + branch_scale * o_ref[pl.ds(r, step), :]
        o_ref[pl.ds(r, step), :] = _layer_norm_rows(y, g, b)
        return carry

    lax.fori_loop(0, rows // step, body, 0)


def _swiglu_step(xb, w1b, w3b, w2b):
    h1 = _dot(xb, w1b)
    h3 = _dot(xb, w3b)
    act = (h1 * jax.nn.sigmoid(h1) * h3).astype(BF16)
    return _dot(act, w2b)


def _ffn_kernel(l_ref, x_ref, xs_ref, w1_ref, w3_ref, w2_ref, c1_ref, c3_ref, c2_ref, *rest, nf,
                n_in_blocks):
    if n_in_blocks:
        cin_ref, g_ref, b_ref, o_ref, os_ref, n1_ref, n3_ref, n2_ref, nin_ref, xb_ref, xsb_ref = rest
    else:
        g_ref, b_ref, o_ref, os_ref, n1_ref, n3_ref, n2_ref, xb_ref, xsb_ref = rest
    i = pl.program_id(0)
    f = pl.program_id(1)

    @pl.when(f == 0)
    def _():
        xb_ref[...] = x_ref[...].astype(BF16)
        o_ref[...] = jnp.zeros_like(o_ref)

    @pl.when((f == 0) & (i == 0))
    def _():
        xsb_ref[...] = xs_ref[...].astype(BF16)
        os_ref[...] = jnp.zeros_like(os_ref)

    n1_ref[...] = c1_ref[...].astype(BF16)
    n3_ref[...] = c3_ref[...].astype(BF16)
    n2_ref[...] = c2_ref[...].astype(BF16)
    if n_in_blocks:
        @pl.when(f < n_in_blocks)
        def _():
            nin_ref[...] = cin_ref[...].astype(BF16)

    o_ref[...] += _swiglu_step(xb_ref[...], w1_ref[...], w3_ref[...], w2_ref[...])

    @pl.when(i == 0)
    def _():
        os_ref[...] += _swiglu_step(xsb_ref[...], w1_ref[...], w3_ref[...], w2_ref[...])

    @pl.when(f == nf - 1)
    def _():
        _residual_ln_epilogue(x_ref, o_ref, g_ref, b_ref, 0.5)

    @pl.when((f == nf - 1) & (i == 0))
    def _():
        _residual_ln_epilogue(xs_ref, os_ref, g_ref, b_ref, 0.5)


def _ffn_ln(layers, x, xs, w1, w3, w2, c1, c3, c2, ln_g, ln_b, ln_idx, tm, c_in=None):
    m, ms = x.shape[0], xs.shape[0]
    nf = FFN_DIM // FFN_TILE_F
    cr = D_MODEL // (m // tm)
    n_in_blocks = 0 if c_in is None else IN_WIDTH // WIDTH_A
    assert n_in_blocks <= nf
    in_block = lambda f: jnp.minimum(f, n_in_blocks - 1)
    in_specs_in = [] if c_in is None else [
        pl.BlockSpec((None, cr, WIDTH_A), lambda i, f, l: (l[0], i, in_block(f)))]
    out_specs_in = [] if c_in is None else [
        pl.BlockSpec((cr, WIDTH_A), lambda i, f, l: (i, in_block(f)))]
    out_shape_in = [] if c_in is None else [jax.ShapeDtypeStruct((D_MODEL, IN_WIDTH), BF16)]
    args_in = [] if c_in is None else [c_in]
    return pl.pallas_call(
        functools.partial(_ffn_kernel, nf=nf, n_in_blocks=n_in_blocks),
        grid_spec=pltpu.PrefetchScalarGridSpec(
            num_scalar_prefetch=1,
            grid=(m // tm, nf),
            in_specs=[
                pl.BlockSpec((tm, D_MODEL), lambda i, f, l: (i, 0)),
                pl.BlockSpec((ms, D_MODEL), lambda i, f, l: (0, 0)),
                pl.BlockSpec((D_MODEL, FFN_TILE_F), lambda i, f, l: (0, f)),
                pl.BlockSpec((D_MODEL, FFN_TILE_F), lambda i, f, l: (0, f)),
                pl.BlockSpec((FFN_TILE_F, D_MODEL), lambda i, f, l: (f, 0)),
                pl.BlockSpec((None, cr, FFN_TILE_F), lambda i, f, l: (l[1], i, f)),
                pl.BlockSpec((None, cr, FFN_TILE_F), lambda i, f, l: (l[1], i, f)),
                pl.BlockSpec((None, FFN_TILE_F, cr), lambda i, f, l: (l[1], f, i)),
            ] + in_specs_in + [
                pl.BlockSpec((None, None, 1, D_MODEL), lambda i, f, l: (l[0], ln_idx, 0, 0)),
                pl.BlockSpec((None, None, 1, D_MODEL), lambda i, f, l: (l[0], ln_idx, 0, 0)),
            ],
            out_specs=[
                pl.BlockSpec((tm, D_MODEL), lambda i, f, l: (i, 0)),
                pl.BlockSpec((ms, D_MODEL), lambda i, f, l: (0, 0)),
                pl.BlockSpec((cr, FFN_TILE_F), lambda i, f, l: (i, f)),
                pl.BlockSpec((cr, FFN_TILE_F), lambda i, f, l: (i, f)),
                pl.BlockSpec((FFN_TILE_F, cr), lambda i, f, l: (f, i)),
            ] + out_specs_in,
            scratch_shapes=[pltpu.VMEM((tm, D_MODEL), BF16), pltpu.VMEM((ms, D_MODEL), BF16)],
        ),
        out_shape=[jax.ShapeDtypeStruct((m, D_MODEL), F32),
                   jax.ShapeDtypeStruct((ms, D_MODEL), F32),
                   jax.ShapeDtypeStruct((D_MODEL, FFN_DIM), BF16),
                   jax.ShapeDtypeStruct((D_MODEL, FFN_DIM), BF16),
                   jax.ShapeDtypeStruct((FFN_DIM, D_MODEL), BF16)] + out_shape_in,
        compiler_params=pltpu.CompilerParams(
            dimension_semantics=("arbitrary", "arbitrary"), vmem_limit_bytes=V7X_VMEM_LIMIT_FFN),
        name="ffn_ln",
    )(layers, x, xs, w1, w3, w2, c1, c3, c2, *args_in, ln_g, ln_b)


def _proj_kernel(l_ref, x_ref, w_ref, o_ref, xb_ref):
    @pl.when(pl.program_id(1) == 0)
    def _():
        xb_ref[...] = x_ref[...].astype(BF16)

    o_ref[...] = _dot(xb_ref[...], w_ref[...])


def _in_proj(layer, x, w_in, tm, tn, n_blocks, skip_after_first=0):
    m = x.shape[0]
    col = lambda n: n + jnp.where(n > 0, skip_after_first, 0)
    return pl.pallas_call(
        _proj_kernel,
        grid_spec=pltpu.PrefetchScalarGridSpec(
            num_scalar_prefetch=1,
            grid=(m // tm, n_blocks),
            in_specs=[
                pl.BlockSpec((tm, D_MODEL), lambda i, n, l: (i, 0)),
                pl.BlockSpec((D_MODEL, tn), lambda i, n, l: (0, col(n))),
            ],
            out_specs=pl.BlockSpec((tm, tn), lambda i, n, l: (i, n)),
            scratch_shapes=[pltpu.VMEM((tm, D_MODEL), BF16)],
        ),
        out_shape=jax.ShapeDtypeStruct((m, n_blocks * tn), F32),
        compiler_params=pltpu.CompilerParams(
            dimension_semantics=("parallel", "arbitrary"), vmem_limit_bytes=V7X_VMEM_LIMIT),
        name="in_proj",
    )(layer, x, w_in)


def _proj_kv_kernel(l_ref, x_ref, wk_ref, wv_ref, k_in_ref, v_in_ref, k_ref, v_ref, kv_ref):
    xb = x_ref[...].astype(BF16)
    for n, (w_ref, dst_ref) in enumerate(((wk_ref, k_ref), (wv_ref, v_ref))):
        res = _dot(xb, w_ref[...])
        for h in range(N_HEADS_A):
            kv_ref[n * N_HEADS_A + h] = res[:, h * HEAD_DIM:(h + 1) * HEAD_DIM]
            dst_ref[:, h, :] = res[:, h * HEAD_DIM:(h + 1) * HEAD_DIM]


def _in_proj_kv(layer, x, w_in, k_all, v_all, tm):
    m = x.shape[0]
    out_spec = pl.BlockSpec((None, tm, N_HEADS_A, HEAD_DIM), lambda i, l: (l[0], i, 0, 0))
    return pl.pallas_call(
        _proj_kv_kernel,
        grid_spec=pltpu.PrefetchScalarGridSpec(
            num_scalar_prefetch=1,
            grid=(m // tm,),
            in_specs=[
                pl.BlockSpec((tm, D_MODEL), lambda i, l: (i, 0)),
                pl.BlockSpec((D_MODEL, WIDTH_A), lambda i, l: (0, 1)),
                pl.BlockSpec((D_MODEL, WIDTH_A), lambda i, l: (0, 2)),
                pl.BlockSpec(memory_space=pl.ANY),
                pl.BlockSpec(memory_space=pl.ANY),
            ],
            out_specs=[out_spec, out_spec,
                       pl.BlockSpec((2 * N_HEADS_A, tm, HEAD_DIM), lambda i, l: (0, i, 0))],
        ),
        out_shape=[jax.ShapeDtypeStruct(k_all.shape, F32), jax.ShapeDtypeStruct(v_all.shape, F32),
                   jax.ShapeDtypeStruct((2 * N_HEADS_A, m, HEAD_DIM), F32)],
        input_output_aliases={4: 0, 5: 1},
        compiler_params=pltpu.CompilerParams(
            dimension_semantics=("parallel",), vmem_limit_bytes=V7X_VMEM_LIMIT),
        name="in_proj_kv",
    )(layer, x, w_in, w_in, k_all, v_all)


def _out_proj_kernel(l_ref, oa_ref, ob_ref, x_ref, w_ref, g_ref, b_ref, o_ref, wb_ref):
    @pl.when(pl.program_id(0) == 0)
    def _():
        wb_ref[...] = w_ref[...].astype(BF16)

    o_ref[...] = _dot(oa_ref[...], wb_ref[0:WIDTH_A, :]) + _dot(ob_ref[...], wb_ref[WIDTH_A:, :])
    _residual_ln_epilogue(x_ref, o_ref, g_ref, b_ref, 1.0)


def _out_proj_ln(layer, oa, ob, x, w_out, ln_g, ln_b, tm):
    m = x.shape[0]
    return pl.pallas_call(
        _out_proj_kernel,
        grid_spec=pltpu.PrefetchScalarGridSpec(
            num_scalar_prefetch=1,
            grid=(m // tm,),
            in_specs=[
                pl.BlockSpec((tm, WIDTH_A), lambda i, l: (i, 0)),
                pl.BlockSpec((tm, WIDTH_BV), lambda i, l: (i, 0)),
                pl.BlockSpec((tm, D_MODEL), lambda i, l: (i, 0)),
                pl.BlockSpec((None, WIDTH_A + WIDTH_BV, D_MODEL), lambda i, l: (l[0], 0, 0),
                             pipeline_mode=pl.Buffered(1)),
                pl.BlockSpec((None, None, 1, D_MODEL), lambda i, l: (l[0], 1, 0, 0)),
                pl.BlockSpec((None, None, 1, D_MODEL), lambda i, l: (l[0], 1, 0, 0)),
            ],
            out_specs=pl.BlockSpec((tm, D_MODEL), lambda i, l: (i, 0)),
            scratch_shapes=[pltpu.VMEM((WIDTH_A + WIDTH_BV, D_MODEL), BF16)],
        ),
        out_shape=jax.ShapeDtypeStruct((m, D_MODEL), F32),
        compiler_params=pltpu.CompilerParams(
            dimension_semantics=("arbitrary",), vmem_limit_bytes=V7X_VMEM_LIMIT),
        name="out_proj_ln",
    )(layer, oa, ob, x, w_out, ln_g, ln_b)


def _ple_kernel(l_ref, x_ref, p_ref, wp_ref, wg_ref, g_ref, b_ref, o_ref, wpb_ref, wgb_ref):
    @pl.when(pl.program_id(0) == 0)
    def _():
        wpb_ref[...] = wp_ref[...].astype(BF16)
        wgb_ref[...] = wg_ref[...].astype(BF16)

    gate = jax.nn.sigmoid(_dot(x_ref[...].astype(BF16), wgb_ref[...]))
    o_ref[...] = _dot(p_ref[...].astype(BF16), wpb_ref[...]) * gate
    _residual_ln_epilogue(x_ref, o_ref, g_ref, b_ref, 1.0)


def _ple_ln(layer, x, p, w_ple, w_gate, ln_g, ln_b, tm):
    m = x.shape[0]
    return pl.pallas_call(
        _ple_kernel,
        grid_spec=pltpu.PrefetchScalarGridSpec(
            num_scalar_prefetch=1,
            grid=(m // tm,),
            in_specs=[
                pl.BlockSpec((tm, D_MODEL), lambda i, l: (i, 0)),
                pl.BlockSpec((None, tm, PLE_DIM), lambda i, l: (l[0], i, 0)),
                pl.BlockSpec((None, PLE_DIM, D_MODEL), lambda i, l: (l[0], 0, 0),
                             pipeline_mode=pl.Buffered(1)),
                pl.BlockSpec((None, D_MODEL, D_MODEL), lambda i, l: (l[0], 0, 0),
                             pipeline_mode=pl.Buffered(1)),
                pl.BlockSpec((None, None, 1, D_MODEL), lambda i, l: (l[0], 3, 0, 0)),
                pl.BlockSpec((None, None, 1, D_MODEL), lambda i, l: (l[0], 3, 0, 0)),
            ],
            out_specs=pl.BlockSpec((tm, D_MODEL), lambda i, l: (i, 0)),
            scratch_shapes=[pltpu.VMEM((PLE_DIM, D_MODEL), BF16),
                            pltpu.VMEM((D_MODEL, D_MODEL), BF16)],
        ),
        out_shape=jax.ShapeDtypeStruct((m, D_MODEL), F32),
        compiler_params=pltpu.CompilerParams(
            dimension_semantics=("arbitrary",), vmem_limit_bytes=V7X_VMEM_LIMIT),
        name="ple_ln",
    )(layer, x, p, w_ple, w_gate, ln_g, ln_b)


def _merge_pair(o_a, l_a, o_b, l_b):
    m = jnp.maximum(l_a, l_b)
    e_a = jnp.exp(l_a - m)
    e_b = jnp.exp(l_b - m)
    den = e_a + e_b
    return (e_a * o_a + e_b * o_b) / den, m + jnp.log(den)


def _attn_prompt_kernel(l_ref, slope_ref, q_ref, k_ref, v_ref, o_ref,
                        x4_ref, x16_ref, va_ref, s_ref, p_ref, ls_ref, bias_ref, og_ref, lse_ref,
                        *, seq):
    nblk = seq // BAND
    slab = ATTN_GROUP * BAND
    part4 = seq // 4
    slope = slope_ref[pl.program_id(1)]

    nat = (q_ref, k_ref, v_ref)
    for t in range(3):
        for a in range(4):
            for c in range(part4 // BAND):
                x4_ref[t, pl.ds(a * part4 + c * BAND, BAND), :] = (
                    nat[t][pl.ds(a + 4 * c * BAND, BAND, stride=4), :])
    for t in range(3):
        for a in range(4):
            for b in range(4):
                x16_ref[t, pl.ds((a + 4 * b) * BAND, BAND), :] = (
                    x4_ref[t, pl.ds(a * part4 + b, BAND, stride=4), :])

    ones = jnp.ones((slab, HEAD_DIM), BF16)
    for slot, src in enumerate((lambda r: v_ref[r, :], lambda r: x4_ref[2, r, :],
                                lambda r: x16_ref[2, r, :])):
        for c in range(seq // slab):
            r = pl.ds(c * slab, slab)
            va_ref[slot, r, 0:HEAD_DIM] = src(r).astype(BF16)
            va_ref[slot, r, HEAD_DIM:2 * HEAD_DIM] = ones

    row = lax.broadcasted_iota(jnp.int32, (slab, 2 * BAND), 0)
    col = lax.broadcasted_iota(jnp.int32, (slab, 2 * BAND), 1)
    dist = jnp.bitwise_and(row, BAND - 1) + BAND - col
    in_band = (dist >= 0) & (dist <= BAND)
    starts_class = (row < BAND) & (col < BAND)
    distf = dist.astype(F32)
    tile1 = jnp.where(in_band, -(slope * (1 * LOG2E)) * distf, NEG_INF)
    bias_ref[1] = tile1
    bias_ref[0] = jnp.where(starts_class, NEG_INF, tile1)
    bias_ref[2] = jnp.where(starts_class | ~in_band, NEG_INF, -(slope * (4 * LOG2E)) * distf)
    dist_own = dist[:, BAND:]
    bias_ref[3, :, 0:BAND] = jnp.where(dist_own >= 0,
                                       -(slope * (16 * LOG2E)) * dist_own.astype(F32), NEG_INF)

    def band_pass(qf, kf, vf, two, first_tile, rest_tile, store):
        kw = 2 * BAND if two else BAND

        if two:
            s_ref[0:BAND, 0:BAND] = jnp.zeros((BAND, BAND), F32)
            s_ref[0:BAND, BAND:kw] = _dot_nt(qf(0, BAND).astype(BF16), kf(0, BAND).astype(BF16))

            def scores(j, carry):
                r = pl.multiple_of(j * BAND, BAND)
                rp = pl.multiple_of(r - BAND, BAND)
                s_ref[pl.ds(r, BAND), :] = _dot_nt(qf(r, BAND).astype(BF16),
                                                   kf(rp, kw).astype(BF16))
                return carry

            lax.fori_loop(1, nblk, scores, 0, unroll=ATTN_UNROLL)
        else:
            def scores(j, carry):
                r = pl.multiple_of(j * BAND, BAND)
                s_ref[pl.ds(r, BAND), 0:kw] = _dot_nt(qf(r, BAND).astype(BF16),
                                                      kf(r, BAND).astype(BF16))
                return carry

            lax.fori_loop(0, nblk, scores, 0, unroll=ATTN_UNROLL)

        def softmax_slab(r, tile):
            s = s_ref[pl.ds(r, slab), 0:kw] * (ATTN_SCALE * LOG2E) + bias_ref[tile, :, 0:kw]
            m = jnp.max(s, axis=-1, keepdims=True)
            p = jnp.exp2(s - m)
            p_ref[pl.ds(r, slab), 0:kw] = p.astype(BF16)
            ls_ref[pl.ds(r, slab), :] = jnp.broadcast_to(m * LN2, (slab, HEAD_DIM))

        first_rest = 0
        if first_tile != rest_tile:
            softmax_slab(0, first_tile)
            first_rest = 1

        def rest_slab(g, carry):
            softmax_slab(pl.multiple_of(g * slab, slab), rest_tile)
            return carry

        lax.fori_loop(first_rest, nblk // ATTN_GROUP, rest_slab, 0, unroll=True)

        def finish(j, r, o_l):
            l = o_l[:, HEAD_DIM:]
            store(j, o_l[:, 0:HEAD_DIM] / l, ls_ref[pl.ds(r, BAND), :] + jnp.log(l))

        if two:
            finish(0, 0, _dot(p_ref[0:BAND, BAND:kw], vf(0, BAND)))

            def values(j, carry):
                r = pl.multiple_of(j * BAND, BAND)
                rp = pl.multiple_of(r - BAND, BAND)
                finish(j, r, _dot(p_ref[pl.ds(r, BAND), :], vf(rp, kw)))
                return carry

            lax.fori_loop(1, nblk, values, 0, unroll=ATTN_UNROLL)
        else:
            def values(j, carry):
                r = pl.multiple_of(j * BAND, BAND)
                finish(j, r, _dot(p_ref[pl.ds(r, BAND), 0:kw], vf(r, BAND)))
                return carry

            lax.fori_loop(0, nblk, values, 0, unroll=ATTN_UNROLL)

    def store_contiguous(slot):
        def store(j, o, lse):
            og_ref[slot, pl.ds(j * BAND, BAND), :] = o
            lse_ref[slot, pl.ds(j * BAND, BAND), :] = lse
        return store

    def store_16_as_4(j, o, lse):
        start = jnp.bitwise_and(j, 3) * part4 + jnp.right_shift(j, 2)
        og_ref[2, pl.ds(start, BAND, stride=4), :] = o
        lse_ref[2, pl.ds(start, BAND, stride=4), :] = lse

    values_of = lambda slot: (lambda start, size: va_ref[slot, pl.ds(start, size), :])
    natural = [lambda start, size, ref=ref: ref[pl.ds(start, size), :] for ref in nat[:2]]
    by4 = [lambda start, size, t=t: x4_ref[t, pl.ds(start, size), :] for t in range(2)]
    by16 = [lambda start, size, t=t: x16_ref[t, pl.ds(start, size), :] for t in range(2)]
    natural.append(values_of(0))
    by4.append(values_of(1))
    by16.append(values_of(2))
    band_pass(*natural, True, 0, 1, store_contiguous(0))
    band_pass(*by4, True, 2, 2, store_contiguous(1))
    band_pass(*by16, False, 3, 3, store_16_as_4)

    for a in range(4):
        for c in range(part4 // BAND):
            src = pl.ds(a * part4 + c * BAND, BAND)
            dst = pl.ds(a + 4 * c * BAND, BAND, stride=4)
            o, lse = _merge_pair(og_ref[1, src, :], lse_ref[1, src, :],
                                 og_ref[2, src, :], lse_ref[2, src, :])
            og_ref[3, dst, :] = o
            lse_ref[3, dst, :] = lse

    def final_merge(c, carry):
        rows = pl.ds(pl.multiple_of(c * BAND, BAND), BAND)
        o, _ = _merge_pair(og_ref[0, rows, :], lse_ref[0, rows, :],
                           og_ref[3, rows, :], lse_ref[3, rows, :])
        o_ref[rows, :] = o.astype(BF16)
        return carry

    lax.fori_loop(0, nblk, final_merge, 0, unroll=True)


def _attn_prompt(layer, zqb, kv, slopes, batch, seq):
    slab = ATTN_GROUP * BAND
    return pl.pallas_call(
        functools.partial(_attn_prompt_kernel, seq=seq),
        grid_spec=pltpu.PrefetchScalarGridSpec(
            num_scalar_prefetch=1,
            grid=(batch, N_HEADS_A),
            in_specs=[
                pl.BlockSpec(memory_space=pltpu.SMEM),
                pl.BlockSpec((seq, HEAD_DIM), lambda b, h, l: (b, h)),
                pl.BlockSpec((None, seq, HEAD_DIM), lambda b, h, l: (h, b, 0)),
                pl.BlockSpec((None, seq, HEAD_DIM), lambda b, h, l: (N_HEADS_A + h, b, 0)),
            ],
            out_specs=pl.BlockSpec((seq, HEAD_DIM), lambda b, h, l: (b, h)),
            scratch_shapes=[
                pltpu.VMEM((3, seq, HEAD_DIM), F32),
                pltpu.VMEM((3, seq, HEAD_DIM), F32),
                pltpu.VMEM((3, seq, 2 * HEAD_DIM), BF16),
                pltpu.VMEM((seq, 2 * BAND), F32),
                pltpu.VMEM((seq, 2 * BAND), BF16),
                pltpu.VMEM((seq, HEAD_DIM), F32),
                pltpu.VMEM((4, slab, 2 * BAND), F32),
                pltpu.VMEM((4, seq, HEAD_DIM), F32),
                pltpu.VMEM((4, seq, HEAD_DIM), F32),
            ],
        ),
        out_shape=jax.ShapeDtypeStruct((batch * seq, WIDTH_A), BF16),
        compiler_params=pltpu.CompilerParams(
            dimension_semantics=("parallel", "parallel"), vmem_limit_bytes=V7X_VMEM_LIMIT),
        name="attn_prompt",
    )(layer, slopes, zqb, kv, kv)


def _group_norm_gate(o, gate, gn_g, gn_b):
    mu = jnp.mean(o, axis=-1, keepdims=True)
    d = o - mu
    var = jnp.mean(d * d, axis=-1, keepdims=True)
    on = d * lax.rsqrt(var + GN_EPS)
    return (on * gn_g + gn_b) * (gate * jax.nn.sigmoid(gate))


def _ret_prompt_kernel(l_ref, lg_ref, q_ref, k_ref, v_ref, gate_ref, gng_ref, gnb_ref, o_ref,
                       s_ref, st_ref, *, seq):
    pair = pl.program_id(1)
    c_len = RET_CHUNK
    lane = lax.broadcasted_iota(jnp.int32, (c_len, 2 * DK_B), 1)
    pos_r = lax.broadcasted_iota(jnp.int32, (c_len, c_len), 0)
    pos_c = lax.broadcasted_iota(jnp.int32, (c_len, c_len), 1)
    rel = (pos_r - pos_c).astype(F32)
    pos = lax.broadcasted_iota(jnp.int32, (c_len, 1), 0).astype(F32)
    ones11 = jnp.ones((1, 1), F32)

    heads = []
    for hx in range(2):
        lg = lg_ref[2 * pair + hx]
        decay_mask = jnp.where(rel >= 0, jnp.exp(jnp.maximum(rel, 0.0) * lg), 0.0)
        q_decay = jnp.exp((pos + 1.0) * lg)
        k_decay = jnp.exp((c_len - 1.0 - pos) * lg)
        chunk_decay = jnp.exp((c_len * ones11) * lg)
        heads.append((decay_mask, q_decay, k_decay, chunk_decay))

    st_ref[...] = jnp.zeros_like(st_ref)

    def chunk(c, carry):
        units = []
        for bi in range(RET_BATCH):
            r = pl.multiple_of(bi * seq + c * c_len, c_len)
            q2 = q_ref[pl.ds(r, c_len), :]
            k2 = k_ref[pl.ds(r, c_len), :] * KB_SCALE
            kb = k2.astype(BF16)
            for hx in range(2):
                qx = jnp.where((lane >= DK_B) if hx else (lane < DK_B), q2, 0.0).astype(BF16)
                units.append((2 * bi + hx, hx, r, qx, kb, k2))
        inners = [_dot_nt(qx, kb) * heads[hx][0] for _, hx, _, qx, kb, _ in units]
        vxs = [v_ref[pl.ds(r, c_len), hx * DV_B:(hx + 1) * DV_B].astype(BF16)
               for _, hx, r, _, _, _ in units]
        states = [st_ref[u] for u, *_ in units]
        intras = [_dot(inner.astype(BF16), vx) for inner, vx in zip(inners, vxs)]
        crosses = [_dot(qx, state.astype(BF16)) for (_, _, _, qx, _, _), state in zip(units, states)]
        updates = [_dot((k2 * heads[hx][2]).T.astype(BF16), vx)
                   for (_, hx, _, _, _, k2), vx in zip(units, vxs)]
        for (u, hx, r, _, _, _), intra, cross, update, state in zip(units, intras, crosses, updates,
                                                                  states):
            _, q_decay, _, chunk_decay = heads[hx]
            cols = slice(hx * DV_B, (hx + 1) * DV_B)
            st_ref[u] = state * chunk_decay + update
            res = _group_norm_gate(intra + cross * q_decay, gate_ref[pl.ds(r, c_len), cols],
                                   gng_ref[:, cols], gnb_ref[:, cols])
            o_ref[pl.ds(r, c_len), cols] = res.astype(BF16)
        return carry

    lax.fori_loop(0, seq // c_len, chunk, 0, unroll=4)
    for bi in range(RET_BATCH):
        s_ref[bi, 0] = st_ref[2 * bi, 0:DK_B, :]
        s_ref[bi, 1] = st_ref[2 * bi + 1, DK_B:2 * DK_B, :]


def _ret_prompt(layer, z, log_g, gn_g, gn_b, batch, seq):
    assert batch % RET_BATCH == 0
    rows = RET_BATCH * seq
    qcol = WIDTH_A // (2 * DK_B)
    kcol = qcol + WIDTH_BQK // (2 * DK_B)
    vcol = (WIDTH_A + 2 * WIDTH_BQK) // (2 * DV_B)
    gcol = vcol + WIDTH_BV // (2 * DV_B)
    return pl.pallas_call(
        functools.partial(_ret_prompt_kernel, seq=seq),
        grid_spec=pltpu.PrefetchScalarGridSpec(
            num_scalar_prefetch=1,
            grid=(batch // RET_BATCH, N_HEADS_B // 2),
            in_specs=[
                pl.BlockSpec(memory_space=pltpu.SMEM),
                pl.BlockSpec((rows, 2 * DK_B), lambda b, p, l: (b, qcol + p)),
                pl.BlockSpec((rows, 2 * DK_B), lambda b, p, l: (b, kcol + p)),
                pl.BlockSpec((rows, 2 * DV_B), lambda b, p, l: (b, vcol + p)),
                pl.BlockSpec((rows, 2 * DV_B), lambda b, p, l: (b, gcol + p)),
                pl.BlockSpec((None, 1, 2 * DV_B), lambda b, p, l: (l[0], 0, p)),
                pl.BlockSpec((None, 1, 2 * DV_B), lambda b, p, l: (l[0], 0, p)),
            ],
            out_specs=[
                pl.BlockSpec((rows, 2 * DV_B), lambda b, p, l: (b, p)),
                pl.BlockSpec((RET_BATCH, 2, DK_B, DV_B), lambda b, p, l: (b, p, 0, 0)),
            ],
            scratch_shapes=[pltpu.VMEM((2 * RET_BATCH, 2 * DK_B, DV_B), F32)],
        ),
        out_shape=[jax.ShapeDtypeStruct((batch * seq, WIDTH_BV), BF16),
                   jax.ShapeDtypeStruct((batch, N_HEADS_B, DK_B, DV_B), F32)],
        compiler_params=pltpu.CompilerParams(
            dimension_semantics=("parallel", "parallel"), vmem_limit_bytes=V7X_VMEM_LIMIT),
        name="ret_prompt",
    )(layer, log_g, z, z, z, z, gn_g, gn_b)


def _attn_decode_kernel(l_ref, slope_ref, z_ref, k1_ref, k4_ref, k16_ref, v1_ref, v4_ref, v16_ref,
                        o_ref):
    row = z_ref[0]
    steps = (BAND - lax.broadcasted_iota(jnp.int32, (BAND, 1), 0)).astype(F32)
    views = ((k1_ref, v1_ref, 1), (k4_ref, v4_ref, 4), (k16_ref, v16_ref, 16))
    for h in range(N_HEADS_A):
        cols = slice(h * HEAD_DIM, (h + 1) * HEAD_DIM)
        qh = row[:, cols]
        k_new = row[:, WIDTH_A + h * HEAD_DIM:WIDTH_A + (h + 1) * HEAD_DIM]
        v_new = row[:, 2 * WIDTH_A + h * HEAD_DIM:2 * WIDTH_A + (h + 1) * HEAD_DIM]
        slope = slope_ref[h]
        s_new = jnp.sum(qh * k_new, axis=1, keepdims=True) * ATTN_SCALE
        outs, lses = [], []
        for k_ref, v_ref, dil in views:
            s = (jnp.sum(k_ref[:, h, :] * qh, axis=1, keepdims=True) * ATTN_SCALE
                 - slope * (steps * dil))
            m = jnp.maximum(jnp.max(s, axis=0, keepdims=True), s_new)
            e = jnp.exp(s - m)
            e_new = jnp.exp(s_new - m)
            l = jnp.sum(e, axis=0, keepdims=True) + e_new
            o = (jnp.sum(e * v_ref[:, h, :], axis=0, keepdims=True) + e_new * v_new) / l
            outs.append(o)
            lses.append(m + jnp.log(l))
        m = jnp.maximum(jnp.maximum(lses[0], lses[1]), lses[2])
        w = [jnp.exp(x - m) for x in lses]
        o = (w[0] * outs[0] + w[1] * outs[1] + w[2] * outs[2]) / (w[0] + w[1] + w[2])
        o_ref[0, :, cols] = o.astype(BF16)


def _attn_decode(layer, slopes, z3, cache_k, cache_v):
    depth, nb, n_past = cache_k.shape[0], cache_k.shape[1], cache_k.shape[2]
    specs, args = [], []
    for cache in (cache_k, cache_v):
        for _, dil in DILATIONS:
            view = cache.reshape(depth, nb, n_past // dil, dil, N_HEADS_A, HEAD_DIM)
            last = n_past // dil // BAND - 1
            specs.append(pl.BlockSpec((None, None, BAND, None, N_HEADS_A, HEAD_DIM),
                                      lambda b, l, last=last: (l[0], b, last, 0, 0, 0)))
            args.append(view)
    return pl.pallas_call(
        _attn_decode_kernel,
        grid_spec=pltpu.PrefetchScalarGridSpec(
            num_scalar_prefetch=1,
            grid=(nb,),
            in_specs=[
                pl.BlockSpec(memory_space=pltpu.SMEM),
                pl.BlockSpec((1, 1, 3 * WIDTH_A), lambda b, l: (b, 0, 0)),
            ] + specs,
            out_specs=pl.BlockSpec((1, 1, WIDTH_A), lambda b, l: (b, 0, 0)),
        ),
        out_shape=jax.ShapeDtypeStruct((nb, 1, WIDTH_A), BF16),
        compiler_params=pltpu.CompilerParams(
            dimension_semantics=("parallel",), vmem_limit_bytes=V7X_VMEM_LIMIT),
        name="attn_decode",
    )(layer, slopes, z3, *args)


def _ret_decode_kernel(l_ref, lg_ref, z_ref, st_ref, gng_ref, gnb_ref, o_ref, sn_ref):
    row = z_ref[0]
    eye = (lax.broadcasted_iota(jnp.int32, (DK_B, DK_B), 0)
           == lax.broadcasted_iota(jnp.int32, (DK_B, DK_B), 1))
    ones11 = jnp.ones((1, 1), F32)
    v0 = 2 * WIDTH_BQK
    g0 = v0 + WIDTH_BV
    for h in range(N_HEADS_B):
        qh = row[:, h * DK_B:(h + 1) * DK_B]
        kh = row[:, WIDTH_BQK + h * DK_B:WIDTH_BQK + (h + 1) * DK_B] * KB_SCALE
        vh = row[:, v0 + h * DV_B:v0 + (h + 1) * DV_B]
        gh = row[:, g0 + h * DV_B:g0 + (h + 1) * DV_B]
        cols = slice(h * DV_B, (h + 1) * DV_B)
        gamma = jnp.exp(ones11 * lg_ref[h])
        q_col = jnp.sum(jnp.where(eye, qh, 0.0), axis=1, keepdims=True)
        k_col = jnp.sum(jnp.where(eye, kh, 0.0), axis=1, keepdims=True)
        state = st_ref[h]
        qk = jnp.sum(qh * kh, axis=1, keepdims=True)
        o = qk * vh + jnp.sum(q_col * state, axis=0, keepdims=True) * gamma
        sn_ref[h] = state * gamma + k_col * vh
        res = _group_norm_gate(o, gh, gng_ref[:, cols], gnb_ref[:, cols])
        o_ref[0, :, cols] = res.astype(BF16)


def _ret_decode(layer, log_g, z3, state_ret, gn_g, gn_b):
    nb = state_ret.shape[1]
    width = 2 * WIDTH_BQK + 2 * WIDTH_BV
    return pl.pallas_call(
        _ret_decode_kernel,
        grid_spec=pltpu.PrefetchScalarGridSpec(
            num_scalar_prefetch=1,
            grid=(nb,),
            in_specs=[
                pl.BlockSpec(memory_space=pltpu.SMEM),
                pl.BlockSpec((1, 1, width), lambda b, l: (b, 0, 1)),
                pl.BlockSpec((None, None, N_HEADS_B, DK_B, DV_B), lambda b, l: (l[0], b, 0, 0, 0)),
                pl.BlockSpec((None, 1, WIDTH_BV), lambda b, l: (l[0], 0, 0)),
                pl.BlockSpec((None, 1, WIDTH_BV), lambda b, l: (l[0], 0, 0)),
            ],
            out_specs=[
                pl.BlockSpec((1, 1, WIDTH_BV), lambda b, l: (b, 0, 0)),
                pl.BlockSpec((None, N_HEADS_B, DK_B, DV_B), lambda b, l: (b, 0, 0, 0)),
            ],
        ),
        out_shape=[jax.ShapeDtypeStruct((nb, 1, WIDTH_BV), BF16),
                   jax.ShapeDtypeStruct((nb, N_HEADS_B, DK_B, DV_B), F32)],
        compiler_params=pltpu.CompilerParams(
            dimension_semantics=("parallel",), vmem_limit_bytes=V7X_VMEM_LIMIT),
        name="ret_decode",
    )(layer, log_g, z3, state_ret, gn_g, gn_b)


def kernel(x_prompt, x_sample, cache_k, cache_v, state_ret, p_prompt, p_sample, w_in, w_out, gn_g,
           gn_b, ffn1_w1, ffn1_w3, ffn1_w2, ffn2_w1, ffn2_w3, ffn2_w2, w_ple, w_gate, ln_g, ln_b):
    batch, seq, _ = x_prompt.shape
    dec_batch, dec_seq, _ = x_sample.shape
    assert dec_seq == 1 and seq == BAND * DILATIONS[-1][1]
    assert cache_k.shape[2] == BAND * DILATIONS[-1][1]
    m_p, m_s = batch * seq, dec_batch * dec_seq
    tm_p = 1024
    assert m_p % tm_p == 0

    f1 = (ffn1_w1, ffn1_w3, ffn1_w2)
    f2 = (ffn2_w1, ffn2_w3, ffn2_w2)
    ln_g4 = ln_g.reshape(DEPTH, 4, 1, D_MODEL)
    ln_b4 = ln_b.reshape(DEPTH, 4, 1, D_MODEL)
    gn_g3 = gn_g.reshape(DEPTH, 1, WIDTH_BV)
    gn_b3 = gn_b.reshape(DEPTH, 1, WIDTH_BV)
    p_p = p_prompt.reshape(DEPTH, m_p, PLE_DIM)
    p_s = p_sample.reshape(DEPTH, m_s, PLE_DIM)

    heads_a = jnp.arange(1, N_HEADS_A + 1, dtype=F32)
    slopes = jnp.exp2(-8.0 * heads_a / N_HEADS_A)
    log_g = jnp.log(1.0 - jnp.exp2(-5.0 - jnp.arange(N_HEADS_B, dtype=F32)))

    def layer_step(carry, i):
        xp, xs, k_p, v_p, f1_b = carry
        layer = i.reshape(1)
        nxt = jnp.minimum(i + 1, DEPTH - 1)

        xp, xs, *f2_b, w_in_b = _ffn_ln(jnp.stack([i, i]), xp, xs, *f1_b, *f2, ln_g4, ln_b4, 0, tm_p,
                                        c_in=w_in)

        zp = _in_proj(layer, xp, w_in_b, tm_p, WIDTH_A, 4, skip_after_first=2)
        k_p, v_p, kv_p = _in_proj_kv(layer, xp, w_in_b, k_p, v_p, 512)
        zs = _in_proj(layer, xs, w_in_b, m_s, WIDTH_A, IN_WIDTH // WIDTH_A)

        oa_p = _attn_prompt(layer, zp, kv_p, slopes, batch, seq)
        ob_p, ret_p = _ret_prompt(layer, zp, log_g, gn_g3, gn_b3, batch, seq)
        zs3 = zs.reshape(m_s, 1, IN_WIDTH)
        oa_s = _attn_decode(layer, slopes, zs3, cache_k, cache_v).reshape(m_s, WIDTH_A)
        ob_s, ret_s = _ret_decode(layer, log_g, zs3, state_ret, gn_g3, gn_b3)
        ob_s = ob_s.reshape(m_s, WIDTH_BV)

        xp = _out_proj_ln(layer, oa_p, ob_p, xp, w_out, ln_g4, ln_b4, 512)
        xs = _out_proj_ln(layer, oa_s, ob_s, xs, w_out, ln_g4, ln_b4, m_s)

        xp, xs, *f1_b = _ffn_ln(jnp.stack([i, nxt]), xp, xs, *f2_b, *f1, ln_g4, ln_b4, 2, tm_p)

        xp = _ple_ln(layer, xp, p_p, w_ple, w_gate, ln_g4, ln_b4, 512)
        xs = _ple_ln(layer, xs, p_s, w_ple, w_gate, ln_g4, ln_b4, m_s)

        k_s = zs[:, WIDTH_A:2 * WIDTH_A].reshape(dec_batch, dec_seq, N_HEADS_A, HEAD_DIM)
        v_s = zs[:, 2 * WIDTH_A:3 * WIDTH_A].reshape(dec_batch, dec_seq, N_HEADS_A, HEAD_DIM)
        return (xp, xs, k_p, v_p, tuple(f1_b)), (ret_p, k_s, v_s, ret_s)

    kv_shape = (DEPTH, m_p, N_HEADS_A, HEAD_DIM)
    (xp, xs, k_p, v_p, _), (ret_p, k_s, v_s, ret_s) = lax.scan(
        layer_step,
        (x_prompt.reshape(m_p, D_MODEL), x_sample.reshape(m_s, D_MODEL),
         jnp.zeros(kv_shape, F32), jnp.zeros(kv_shape, F32),
         tuple(w[0].astype(BF16) for w in f1)),
        jnp.arange(DEPTH, dtype=jnp.int32))
    kv_out = (DEPTH, batch, seq, N_HEADS_A, HEAD_DIM)
    return (xp.reshape(batch, seq, D_MODEL), xs.reshape(dec_batch, dec_seq, D_MODEL),
            k_p.reshape(kv_out), v_p.reshape(kv_out), ret_p, k_s, v_s, ret_s)
```
